```python
import jax
import jax.numpy as jnp
from jax import lax
import numpy as np

D_MODEL = 1024
BATCH = 4
SEQ = 4096
DEPTH = 2
DEC_BATCH = 128
DEC_SEQ = 8
PAST_LEN = 16384
PAGE_SIZE = 128

N_META = 16
BLOCK = 128
EPS = 1e-6
MLA_HEADS = 8
MLA_NOPE = 64
MLA_ROPE = 32
MLA_V = 64
Q_LORA = 384
KV_LORA = 256
ROPE_BASE = 10000.0
MLA_SCALE = (MLA_NOPE + MLA_ROPE) ** -0.5
CONV_DIM = 512
CONV_K = 3
SB_Q_HEADS = 8
SB_KV_HEADS = 2
SB_GROUP = SB_Q_HEADS // SB_KV_HEADS
SB_HEAD_DIM = 64
SB_SCALE = SB_HEAD_DIM ** -0.5
D_FF = 4 * D_MODEL
SPLIT_SIZES = (Q_LORA, KV_LORA, MLA_ROPE, CONV_DIM, CONV_DIM, CONV_DIM,
               SB_Q_HEADS * SB_HEAD_DIM, SB_KV_HEADS * SB_HEAD_DIM, SB_KV_HEADS * SB_HEAD_DIM,
               D_MODEL, D_MODEL, D_MODEL)
D_IN = sum(SPLIT_SIZES)

kernel_name = 'hybrid_mla_shortconv_stickbreak_step'


def rmsnorm(x, g):
    xf = x.astype(jnp.float32)
    y = xf * lax.rsqrt(jnp.mean(xf * xf, axis=-1, keepdims=True) + EPS)
    return (y * g.astype(jnp.float32)).astype(x.dtype)


def split_cols(z):
    parts, off = [], 0
    for n in SPLIT_SIZES:
        parts.append(z[..., off:off + n])
        off += n
    return parts


def rope_cos_sin(pos):
    inv = ROPE_BASE ** (-jnp.arange(0, MLA_ROPE, 2, dtype=jnp.float32) / MLA_ROPE)
    ang = pos.astype(jnp.float32)[:, None] * inv[None, :]
    return jnp.cos(ang), jnp.sin(ang)


def rotary(x, cos, sin):
    half = x.shape[-1] // 2
    xf = x.astype(jnp.float32)
    x1, x2 = xf[..., :half], xf[..., half:]
    return jnp.concatenate([x1 * cos - x2 * sin, x2 * cos + x1 * sin], axis=-1).astype(x.dtype)


def mla_keys_nope(c, w_uk, g):
    return rmsnorm(jnp.einsum('...r,rhd->...hd', c, w_uk), g)


def mla_attend(qn, qr, kn, kr, c, qpos, kpos, w_uv):
    s = (jnp.einsum('qhd,khd->hqk', qn, kn, preferred_element_type=jnp.float32)
         + jnp.einsum('qhd,kd->hqk', qr, kr, preferred_element_type=jnp.float32)) * MLA_SCALE
    mask = (kpos[None, :] >= 0) & (kpos[None, :] <= qpos[:, None])
    p = jax.nn.softmax(jnp.where(mask, s, jnp.finfo(jnp.float32).min), axis=-1)
    lat = jnp.einsum('hqk,kr->qhr', p.astype(c.dtype), c)
    return jnp.einsum('qhr,rhd->qhd', lat, w_uv)


def sb_attend(q, k, v, qpos, kpos):
    z = jnp.einsum('qhgd,khd->hgqk', q, k, preferred_element_type=jnp.float32) * SB_SCALE
    mask = (kpos[None, :] >= 0) & (kpos[None, :] < qpos[:, None])
    log_1m = jnp.where(mask, jax.nn.log_sigmoid(-z), 0.0)
    tail = lax.cumsum(log_1m, axis=3, reverse=True) - log_1m
    a = jnp.where(mask, jnp.exp(jax.nn.log_sigmoid(z) + tail), 0.0)
    return jnp.einsum('hgqk,khd->qhgd', a.astype(v.dtype), v)


def pre_mix(x, pos, norm_g, w_in, q_a_norm, w_uq, kv_a_norm, q_norm_nope, q_norm_rope, k_norm_rope):
    bsz, t = x.shape[:2]
    h = rmsnorm(x, norm_g)
    (q_lat, c_raw, kr_raw, b_in, c_in, x_in, sq, sk, sv, ga, gb, gc) = split_cols(h @ w_in)
    cos, sin = rope_cos_sin(pos)
    q = (rmsnorm(q_lat, q_a_norm) @ w_uq).reshape(bsz, t, MLA_HEADS, MLA_NOPE + MLA_ROPE)
    qn = rmsnorm(q[..., :MLA_NOPE], q_norm_nope)
    qr = rotary(rmsnorm(q[..., MLA_NOPE:], q_norm_rope), cos[:, None], sin[:, None])
    c = rmsnorm(c_raw, kv_a_norm)
    kr = rotary(rmsnorm(kr_raw, k_norm_rope), cos, sin)
    u = c_in * x_in
    sq = sq.reshape(bsz, t, SB_KV_HEADS, SB_GROUP, SB_HEAD_DIM)
    sk = sk.reshape(bsz, t, SB_KV_HEADS, SB_HEAD_DIM)
    sv = sv.reshape(bsz, t, SB_KV_HEADS, SB_HEAD_DIM)
    gates = (jax.nn.sigmoid(ga), jax.nn.sigmoid(gb), jax.nn.sigmoid(gc))
    return qn, qr, c, kr, b_in, u, sq, sk, sv, gates


def short_conv(u, prev, w):
    t = u.shape[1]
    hist = jnp.concatenate([prev, u], axis=1)
    y = w[0] * hist[:, 0:t]
    for j in range(1, CONV_K):
        y = y + w[j] * hist[:, j:j + t]
    return y, hist[:, -(CONV_K - 1):]


def post_mix(x, o_mla, y_conv, o_sb, gates, w_pa, w_pb, w_pc, w_o, norm_ffn, w_ff1, w_ff2):
    bsz, t = x.shape[:2]
    ya = o_mla.reshape(bsz, t, -1) @ w_pa
    yb = y_conv @ w_pb
    yc = o_sb.reshape(bsz, t, -1) @ w_pc
    ga, gb, gc = gates
    x = x + (ga * ya + gb * yb + gc * yc) @ w_o
    h = rmsnorm(x, norm_ffn)
    return x + jnp.square(jax.nn.relu(h @ w_ff1)) @ w_ff2


def prompt_attention(qn, qr, kn, kr, c, sq, sk, sv, w_uv):
    bsz, seq_len = qn.shape[:2]
    n_pad = (-seq_len) % BLOCK
    n_blk = (seq_len + n_pad) // BLOCK
    pos = jnp.arange(seq_len + n_pad) - n_pad

    def pad(a):
        return jnp.pad(a, [(0, 0), (n_pad, 0)] + [(0, 0)] * (a.ndim - 2))

    def to_blocks(a):
        return jnp.moveaxis(a.reshape((bsz, n_blk, BLOCK) + a.shape[2:]), 1, 0)

    def from_blocks(a):
        return jnp.moveaxis(a, 0, 1).reshape((bsz, n_blk * BLOCK) + a.shape[3:])[:, n_pad:]

    qn, qr, kn, kr, c, sq, sk, sv = (pad(a) for a in (qn, qr, kn, kr, c, sq, sk, sv))
    mla_b = jax.vmap(mla_attend, in_axes=(0, 0, 0, 0, 0, None, None, None))
    sb_b = jax.vmap(sb_attend, in_axes=(0, 0, 0, None, None))

    def one_block(args):
        qn_b, qr_b, sq_b, qpos_b = args
        return (mla_b(qn_b, qr_b, kn, kr, c, qpos_b, pos, w_uv),
                sb_b(sq_b, sk, sv, qpos_b, pos))

    o_m, o_s = lax.map(one_block, (to_blocks(qn), to_blocks(qr), to_blocks(sq),
                                   pos.reshape(n_blk, BLOCK)))
    return from_blocks(o_m), from_blocks(o_s)


def sample_attention(layer, qn, qr, c, kr, sq, sk, sv, cache_lat, cache_rope, cache_k, cache_v,
                     page_table, w_uk, k_norm_nope, w_uv):
    t = qn.shape[1]
    qpos = PAST_LEN + jnp.arange(t)
    kpos = jnp.arange(PAST_LEN + t)

    def gather(cache, pt):
        return cache[layer, pt].reshape((-1,) + cache.shape[3:])

    def one_seq(args):
        qn_b, qr_b, c_b, kr_b, sq_b, sk_b, sv_b, pt = args
        c_all = jnp.concatenate([gather(cache_lat, pt), c_b], axis=0)
        kr_all = jnp.concatenate([gather(cache_rope, pt), kr_b], axis=0)
        k_all = jnp.concatenate([gather(cache_k, pt), sk_b], axis=0)
        v_all = jnp.concatenate([gather(cache_v, pt), sv_b], axis=0)
        o_m = mla_attend(qn_b, qr_b, mla_keys_nope(c_all, w_uk, k_norm_nope), kr_all, c_all,
                         qpos, kpos, w_uv)
        o_s = sb_attend(sq_b, k_all, v_all, qpos, kpos)
        return o_m, o_s

    return lax.map(one_seq, (qn, qr, c, kr, sq, sk, sv, page_table))


def setup_inputs(seed: int = 0) -> dict:
    key = jax.random.key(seed)
    ks = iter(jax.random.split(key, 40))
    n_pages = PAST_LEN // PAGE_SIZE
    n_used = DEC_BATCH * n_pages
    n_phys = n_used + n_used // 4

    def nrm(shape, scale=1.0):
        return scale * jax.random.normal(next(ks), shape, jnp.float32)

    def gain(n):
        return 1.0 + 0.02 * jax.random.normal(next(ks), (DEPTH, n), jnp.float32)

    x_prompt = nrm((BATCH, SEQ, D_MODEL))
    x_sample = nrm((DEC_BATCH, DEC_SEQ, D_MODEL))
    cache_mla_latent = nrm((DEPTH, n_phys, PAGE_SIZE, KV_LORA))
    cache_mla_rope = nrm((DEPTH, n_phys, PAGE_SIZE, MLA_ROPE))
    cache_sb_k = nrm((DEPTH, n_phys, PAGE_SIZE, SB_KV_HEADS, SB_HEAD_DIM))
    cache_sb_v = nrm((DEPTH, n_phys, PAGE_SIZE, SB_KV_HEADS, SB_HEAD_DIM))
    state_conv = nrm((DEPTH, DEC_BATCH, CONV_K - 1, CONV_DIM))
    page_table = jax.random.permutation(next(ks), n_phys)[:n_used].reshape(
        DEC_BATCH, n_pages).astype(jnp.int32)
    meta_tokens = nrm((N_META, D_MODEL))
    norm_mix = gain(D_MODEL)
    w_in = nrm((DEPTH, D_MODEL, D_IN), D_MODEL ** -0.5)
    q_a_norm = gain(Q_LORA)
    w_uq = nrm((DEPTH, Q_LORA, MLA_HEADS * (MLA_NOPE + MLA_ROPE)), Q_LORA ** -0.5)
    kv_a_norm = gain(KV_LORA)
    w_uk = nrm((DEPTH, KV_LORA, MLA_HEADS, MLA_NOPE), KV_LORA ** -0.5)
    w_uv = nrm((DEPTH, KV_LORA, MLA_HEADS, MLA_V), KV_LORA ** -0.5)
    q_norm_nope = gain(MLA_NOPE)
    q_norm_rope = gain(MLA_ROPE)
    k_norm_nope = gain(MLA_NOPE)
    k_norm_rope = gain(MLA_ROPE)
    conv_w = nrm((DEPTH, CONV_K, CONV_DIM), CONV_K ** -0.5)
    w_pa = nrm((DEPTH, MLA_HEADS * MLA_V, D_MODEL), (MLA_HEADS * MLA_V) ** -0.5)
    w_pb = nrm((DEPTH, CONV_DIM, D_MODEL), CONV_DIM ** -0.5)
    w_pc = nrm((DEPTH, SB_Q_HEADS * SB_HEAD_DIM, D_MODEL), (SB_Q_HEADS * SB_HEAD_DIM) ** -0.5)
    w_o = nrm((DEPTH, D_MODEL, D_MODEL), D_MODEL ** -0.5)
    norm_ffn = gain(D_MODEL)
    w_ff1 = nrm((DEPTH, D_MODEL, D_FF), D_MODEL ** -0.5)
    w_ff2 = nrm((DEPTH, D_FF, D_MODEL), D_FF ** -0.5)
    return {'x_prompt': x_prompt, 'x_sample': x_sample,
            'cache_mla_latent': cache_mla_latent, 'cache_mla_rope': cache_mla_rope,
            'cache_sb_k': cache_sb_k, 'cache_sb_v': cache_sb_v, 'state_conv': state_conv,
            'page_table': page_table, 'meta_tokens': meta_tokens, 'norm_mix': norm_mix,
            'w_in': w_in, 'q_a_norm': q_a_norm, 'w_uq': w_uq, 'kv_a_norm': kv_a_norm,
            'w_uk': w_uk, 'w_uv': w_uv, 'q_norm_nope': q_norm_nope, 'q_norm_rope': q_norm_rope,
            'k_norm_nope': k_norm_nope, 'k_norm_rope': k_norm_rope, 'conv_w': conv_w,
            'w_pa': w_pa, 'w_pb': w_pb, 'w_pc': w_pc, 'w_o': w_o, 'norm_ffn': norm_ffn,
            'w_ff1': w_ff1, 'w_ff2': w_ff2}


def reference(x_prompt, x_sample, cache_mla_latent, cache_mla_rope, cache_sb_k, cache_sb_v,
              state_conv, page_table, meta_tokens, norm_mix, w_in, q_a_norm, w_uq, kv_a_norm,
              w_uk, w_uv, q_norm_nope, q_norm_rope, k_norm_nope, k_norm_rope, conv_w,
              w_pa, w_pb, w_pc, w_o, norm_ffn, w_ff1, w_ff2):
    bsz = x_prompt.shape[0]
    meta = jnp.broadcast_to(meta_tokens[None].astype(x_prompt.dtype), (bsz, N_META, D_MODEL))
    xp = jnp.concatenate([meta, x_prompt], axis=1)
    xs = x_sample
    pos_p = jnp.arange(xp.shape[1])
    pos_s = PAST_LEN + jnp.arange(xs.shape[1])
    p_lat, p_rope, p_k, p_v, p_conv = [], [], [], [], []
    s_lat, s_rope, s_k, s_v, s_conv = [], [], [], [], []
    for l in range(DEPTH):
        qn, qr, c, kr, b_in, u, sq, sk, sv, gates = pre_mix(
            xp, pos_p, norm_mix[l], w_in[l], q_a_norm[l], w_uq[l], kv_a_norm[l],
            q_norm_nope[l], q_norm_rope[l], k_norm_rope[l])
        kn = mla_keys_nope(c, w_uk[l], k_norm_nope[l])
        o_m, o_s = prompt_attention(qn, qr, kn, kr, c, sq, sk, sv, w_uv[l])
        y_c, conv_p = short_conv(u, jnp.zeros((bsz, CONV_K - 1, CONV_DIM), u.dtype), conv_w[l])
        xp = post_mix(xp, o_m, b_in * y_c, o_s, gates, w_pa[l], w_pb[l], w_pc[l], w_o[l],
                      norm_ffn[l], w_ff1[l], w_ff2[l])
        p_lat.append(c)
        p_rope.append(kr)
        p_k.append(sk)
        p_v.append(sv)
        p_conv.append(conv_p)
        qn, qr, c, kr, b_in, u, sq, sk, sv, gates = pre_mix(
            xs, pos_s, norm_mix[l], w_in[l], q_a_norm[l], w_uq[l], kv_a_norm[l],
            q_norm_nope[l], q_norm_rope[l], k_norm_rope[l])
        o_m, o_s = sample_attention(l, qn, qr, c, kr, sq, sk, sv, cache_mla_latent,
                                    cache_mla_rope, cache_sb_k, cache_sb_v, page_table,
                                    w_uk[l], k_norm_nope[l], w_uv[l])
        y_c, conv_s = short_conv(u, state_conv[l], conv_w[l])
        xs = post_mix(xs, o_m, b_in * y_c, o_s, gates, w_pa[l], w_pb[l], w_pc[l], w_o[l],
                      norm_ffn[l], w_ff1[l], w_ff2[l])
        s_lat.append(c)
        s_rope.append(kr)
        s_k.append(sk)
        s_v.append(sv)
        s_conv.append(conv_s)
    y_prompt = xp[:, N_META:]
    y_sample = xs
    new_prompt_latent = jnp.stack(p_lat)
    new_prompt_rope = jnp.stack(p_rope)
    new_prompt_sb_k = jnp.stack(p_k)
    new_prompt_sb_v = jnp.stack(p_v)
    new_prompt_conv = jnp.stack(p_conv)
    new_sample_latent = jnp.stack(s_lat)
    new_sample_rope = jnp.stack(s_rope)
    new_sample_sb_k = jnp.stack(s_k)
    new_sample_sb_v = jnp.stack(s_v)
    new_sample_conv = jnp.stack(s_conv)
    return (y_prompt, y_sample, new_prompt_latent, new_prompt_rope, new_prompt_sb_k,
            new_prompt_sb_v, new_prompt_conv, new_sample_latent, new_sample_rope,
            new_sample_sb_k, new_sample_sb_v, new_sample_conv)
```

```python
import functools

import numpy as np
import jax
import jax.numpy as jnp
from jax import lax
from jax.experimental import pallas as pl
from jax.experimental.pallas import tpu as pltpu

N_META = 16
EPS = 1e-6
MLA_HEADS = 8
MLA_NOPE = 64
MLA_ROPE = 32
MLA_V = 64
Q_LORA = 384
KV_LORA = 256
ROPE_BASE = 10000.0
MLA_SCALE = (MLA_NOPE + MLA_ROPE) ** -0.5
CONV_DIM = 512
CONV_K = 3
SB_Q_HEADS = 8
SB_KV_HEADS = 2
SB_GROUP = SB_Q_HEADS // SB_KV_HEADS
SB_HEAD_DIM = 64
SB_SCALE = SB_HEAD_DIM ** -0.5
SPLIT_SIZES = (Q_LORA, KV_LORA, MLA_ROPE, CONV_DIM, CONV_DIM, CONV_DIM,
               SB_Q_HEADS * SB_HEAD_DIM, SB_KV_HEADS * SB_HEAD_DIM, SB_KV_HEADS * SB_HEAD_DIM)

LANE = 128
SUBLANE = 8
HEAD_PAD = LANE
ROW_TILE = 256
NEG_BIG = -1e30
VMEM_LIMIT = 56 * 1024 * 1024

F32 = jnp.float32
BF16 = jnp.bfloat16

_A_QLAT = 0
_A_C = _A_QLAT + Q_LORA
_A_KR = _A_C + KV_LORA
_A_KRP = _A_KR + HEAD_PAD
_A_B = _A_KRP + HEAD_PAD
_A_CI = _A_B + CONV_DIM
_A_X = _A_CI + CONV_DIM
_A_SQ = _A_X + CONV_DIM
_A_SK = _A_SQ + SB_Q_HEADS * SB_HEAD_DIM
_A_SV = _A_SK + SB_KV_HEADS * SB_HEAD_DIM
_A_END = _A_SV + SB_KV_HEADS * SB_HEAD_DIM
_QW = MLA_HEADS * HEAD_PAD


def _dot(a, b):
    return jnp.dot(a, b, preferred_element_type=F32)


def _dot_nt(a, b):
    return lax.dot_general(a, b, (((1,), (1,)), ((), ())), preferred_element_type=F32)


def _rms(x, g):
    return x * lax.rsqrt(jnp.mean(x * x, axis=-1, keepdims=True) + EPS) * g


def _split_hi_lo(x):
    hi = x.astype(BF16)
    lo = (x - hi.astype(F32)).astype(BF16)
    return hi, lo


def _group_rs(raw, g_ref, gt2_ref):
    ms = _dot((raw * raw).astype(BF16), g_ref[...])
    rs = lax.rsqrt(ms + EPS)
    hi, lo = _split_hi_lo(rs)
    return _dot(jnp.concatenate([hi, lo], axis=1), gt2_ref[...])


def _premix_kernel(*refs, sample, tm):
    if sample:
        (x_ref, cos_ref, sin_ref, p1_ref, p2_ref, gmix_ref, wa_ref, gqa_ref, wuq_ref, gq_ind, gqt_ind,
         gqm_ref, gqp_ref, gkva_ref, gkr_ref, gkrp_ref, convw_ref, gk_ref, wabs_ref,
         q_out, qabs_out, c_out, kr_out, sk_out, sv_out, sq_out, yb_out, u_out) = refs
    else:
        (x_ref, cos_ref, sin_ref, gmix_ref, wa_ref, gqa_ref, wuq_ref, gq_ind, gqt_ind,
         gqm_ref, gqp_ref, gkva_ref, gkr_ref, gkrp_ref, convw_ref, gk_ref, wuk_ref, gk_ind, gkt_ind, wuv_ref,
         q_out, k_out, v_out, c_out, kr_out, sk_out, sv_out, sq_out, yb_out, utail_out, carry_ref) = refs

    x = x_ref[...]
    h = _rms(x, gmix_ref[...]).astype(BF16)

    def proj(a, b):
        return _dot(h, wa_ref[:, a:b])

    cos = cos_ref[...]
    sin = sin_ref[...]

    ql = _rms(proj(_A_QLAT, _A_C), gqa_ref[...]).astype(BF16)
    qraw = _dot(ql, wuq_ref[...])
    qm = qraw[:, :_QW]
    qp = qraw[:, _QW:]
    rs_full = _group_rs(qm, gq_ind, gqt_ind)
    ym = qm * rs_full * gqm_ref[...]
    yp = qp * rs_full * gqp_ref[...]
    q_heads = []
    for hd in range(MLA_HEADS):
        sl = slice(hd * HEAD_PAD, (hd + 1) * HEAD_PAD)
        q_heads.append(ym[:, sl] * cos + yp[:, sl] * sin)
    q_full = jnp.concatenate(q_heads, axis=1)
    q_out[...] = q_full.astype(q_out.dtype)

    c = _rms(proj(_A_C, _A_KR), gkva_ref[...])
    c_out[...] = c
    c_bf = c.astype(BF16)
    kr_raw = proj(_A_KR, _A_KRP)
    kr_par = proj(_A_KRP, _A_B)
    kr_rs = lax.rsqrt(jnp.sum(kr_raw * kr_raw, axis=-1, keepdims=True) * (1.0 / MLA_ROPE) + EPS)
    kr128 = (kr_raw * kr_rs * gkr_ref[...]) * cos + (kr_par * kr_rs * gkrp_ref[...]) * sin
    kr_out[...] = kr128

    if sample:
        qabs_out[...] = _dot((q_full * gk_ref[...]).astype(BF16), wabs_ref[...])
    else:
        kraw = _dot(c_bf, wuk_ref[...])
        kn = kraw * _group_rs(kraw, gk_ind, gkt_ind) * gk_ref[...]
        for hd in range(MLA_HEADS):
            sl = slice(hd * HEAD_PAD, (hd + 1) * HEAD_PAD)
            k_out[:, sl] = (kn[:, sl] + kr128).astype(k_out.dtype)
        v_out[...] = _dot(c_bf, wuv_ref[...]).astype(v_out.dtype)

    sq_out[...] = (proj(_A_SQ, _A_SK) * SB_SCALE).astype(sq_out.dtype)
    sk_out[...] = proj(_A_SK, _A_SV)
    sv_out[...] = proj(_A_SV, _A_END)

    u = proj(_A_CI, _A_X) * proj(_A_X, _A_SQ)
    r1 = pltpu.roll(u, 1, 0)
    r2 = pltpu.roll(u, 2, 0)
    row = lax.broadcasted_iota(jnp.int32, u.shape, 0)
    if sample:
        in_seq = row % SUBLANE
        prev1 = jnp.where(in_seq == 0, p1_ref[...], r1)
        prev2 = jnp.where(in_seq < 2, p2_ref[...], r2)
        u_out[...] = u
    else:
        @pl.when(pl.program_id(0) == 0)
        def _():
            carry_ref[...] = jnp.zeros_like(carry_ref)
        last1 = carry_ref[SUBLANE - 1:SUBLANE, :]
        last2 = carry_ref[SUBLANE - 2:SUBLANE - 1, :]
        prev1 = jnp.where(row == 0, last1, r1)
        prev2 = jnp.where(row == 0, last2, jnp.where(row == 1, last1, r2))
        tail = u[tm - SUBLANE:, :]
        carry_ref[...] = tail
        utail_out[...] = tail
    w = convw_ref[...]
    y_conv = w[0:1, :] * prev2 + w[1:2, :] * prev1 + w[2:3, :] * u
    yb_out[...] = (proj(_A_B, _A_CI) * y_conv).astype(yb_out.dtype)


def _full_spec(a):
    nd = a.ndim
    return pl.BlockSpec(a.shape, lambda i, _nd=nd: (0,) * _nd)


def _row_spec(tm, width):
    return pl.BlockSpec((tm, width), lambda i: (i, 0))


def _premix(x, cos, sin, wts, sample, p1=None, p2=None):
    m = x.shape[0]
    tm = ROW_TILE
    d_model = x.shape[1]
    nblk = m // tm
    row = lambda w_: _row_spec(tm, w_)
    common_w = [wts['gmix'], wts['wa'], wts['gqa'], wts['wuq'], wts['gq_ind'], wts['gqt_ind'],
                wts['gqm'], wts['gqp'], wts['gkva'], wts['gkr'], wts['gkrp'], wts['convw'], wts['gk']]
    if sample:
        ins = [x, cos, sin, p1, p2] + common_w + [wts['wabs']]
        in_specs = ([row(d_model), row(HEAD_PAD), row(HEAD_PAD), row(CONV_DIM), row(CONV_DIM)]
                    + [_full_spec(a) for a in common_w + [wts['wabs']]])
        outs = [(_QW, F32), (MLA_HEADS * KV_LORA, F32), (KV_LORA, F32), (HEAD_PAD, F32),
                (SB_KV_HEADS * SB_HEAD_DIM, F32), (SB_KV_HEADS * SB_HEAD_DIM, F32),
                (SB_Q_HEADS * SB_HEAD_DIM, F32), (CONV_DIM, BF16), (CONV_DIM, F32)]
        out_shape = [jax.ShapeDtypeStruct((m, w_), dt) for w_, dt in outs]
        out_specs = [row(w_) for w_, _ in outs]
        scratch = []
    else:
        extra = [wts['wuk'], wts['gk_ind'], wts['gkt_ind'], wts['wuv']]
        ins = [x, cos, sin] + common_w + extra
        in_specs = ([row(d_model), row(HEAD_PAD), row(HEAD_PAD)] + [_full_spec(a) for a in common_w + extra])
        outs = [(_QW, BF16), (_QW, BF16), (MLA_HEADS * MLA_V, BF16), (KV_LORA, F32), (HEAD_PAD, F32),
                (SB_KV_HEADS * SB_HEAD_DIM, F32), (SB_KV_HEADS * SB_HEAD_DIM, F32),
                (SB_Q_HEADS * SB_HEAD_DIM, BF16), (CONV_DIM, BF16)]
        out_shape = [jax.ShapeDtypeStruct((m, w_), dt) for w_, dt in outs]
        out_shape.append(jax.ShapeDtypeStruct((nblk * SUBLANE, CONV_DIM), F32))
        out_specs = [row(w_) for w_, _ in outs] + [pl.BlockSpec((SUBLANE, CONV_DIM), lambda i: (i, 0))]
        scratch = [pltpu.VMEM((SUBLANE, CONV_DIM), F32)]
    return pl.pallas_call(
        functools.partial(_premix_kernel, sample=sample, tm=tm),
        grid=(nblk,),
        in_specs=in_specs,
        out_specs=out_specs,
        out_shape=out_shape,
        scratch_shapes=scratch,
        compiler_params=pltpu.CompilerParams(dimension_semantics=("arbitrary",),
                                             vmem_limit_bytes=VMEM_LIMIT),
    )(*ins)


def _mla_prompt_kernel(q_ref, k_ref, v_ref, o_ref, *, tq, tk, n_pad):
    qi = pl.program_id(2)
    row = qi * tq + lax.broadcasted_iota(jnp.int32, (tq, tk), 0)
    col0 = lax.broadcasted_iota(jnp.int32, (tq, tk), 1)
    accs = []
    for hh in range(2):
        sl = slice(hh * HEAD_PAD, (hh + 1) * HEAD_PAD)
        q = q_ref[:, sl]

        def body(j, carry, sl=sl, q=q):
            m, l, acc = carry
            ks = pl.multiple_of(j * tk, tk)
            k = k_ref[pl.ds(ks, tk), sl]
            v = v_ref[pl.ds(ks, tk), :]
            s = _dot_nt(q, k)
            col = col0 + j * tk
            s = jnp.where((col >= n_pad) & (col <= row), s, NEG_BIG)
            m_new = jnp.maximum(m, jnp.max(s, axis=-1, keepdims=True))
            p = jnp.exp(s - m_new)
            alpha = jnp.exp(m - m_new)
            l = alpha * l + jnp.sum(p, axis=-1, keepdims=True)
            acc = alpha * acc + _dot(p.astype(BF16), v)
            return m_new, l, acc

        init = (jnp.full((tq, 1), NEG_BIG, F32), jnp.zeros((tq, 1), F32), jnp.zeros((tq, 2 * MLA_V), F32))
        _, l, acc = lax.fori_loop(0, qi + 1, body, init)
        accs.append(acc / l)
    lane = lax.broadcasted_iota(jnp.int32, (tq, 2 * MLA_V), 1)
    o_ref[...] = jnp.where(lane < MLA_V, accs[0], accs[1]).astype(o_ref.dtype)


def _mla_prompt(q, k, v, bsz, lp, n_pad):
    tq = tk = ROW_TILE
    nq = lp // tq
    q3 = q.reshape(bsz, lp, _QW)
    k3 = k.reshape(bsz, lp, _QW)
    v3 = v.reshape(bsz, lp, MLA_HEADS * MLA_V)
    out = pl.pallas_call(
        functools.partial(_mla_prompt_kernel, tq=tq, tk=tk, n_pad=n_pad),
        grid=(bsz, MLA_HEADS // 2, nq),
        in_specs=[pl.BlockSpec((None, tq, 2 * HEAD_PAD), lambda b, hp, i: (b, i, hp)),
                  pl.BlockSpec((None, lp, 2 * HEAD_PAD), lambda b, hp, i: (b, 0, hp)),
                  pl.BlockSpec((None, lp, 2 * MLA_V), lambda b, hp, i: (b, 0, hp))],
        out_specs=pl.BlockSpec((None, tq, 2 * MLA_V), lambda b, hp, i: (b, i, hp)),
        out_shape=jax.ShapeDtypeStruct((bsz, lp, MLA_HEADS * MLA_V), BF16),
        compiler_params=pltpu.CompilerParams(dimension_semantics=("arbitrary", "arbitrary", "arbitrary"),
                                             vmem_limit_bytes=VMEM_LIMIT),
    )(q3, k3, v3)
    return out.reshape(bsz * lp, MLA_HEADS * MLA_V)


def _neg_softplus(z):
    return -(jnp.maximum(z, 0.0) + jnp.log(1.0 + jnp.exp(-jnp.abs(z))))


def _suffix_matrix(t):
    return (lax.broadcasted_iota(jnp.int32, (t, t), 0) > lax.broadcasted_iota(jnp.int32, (t, t), 1)).astype(BF16)


def _sb_prompt_kernel(q_ref, k_ref, v_ref, o_ref, k2_ref, v2_ref, *, tq, tk, n_pad):
    kvh = pl.program_id(1)
    qi = pl.program_id(2)
    d = SB_HEAD_DIM

    @pl.when(qi == 0)
    def _():
        kk = k_ref[...]
        vv = v_ref[...]
        kroll = pltpu.roll(kk, d, 1)
        vroll = pltpu.roll(vv, d, 1)
        lane = lax.broadcasted_iota(jnp.int32, kk.shape, 1)
        first = kvh == 0
        low = lane < d
        k2_ref[:, :LANE] = jnp.where(low, jnp.where(first, kk, kroll), 0.0).astype(BF16)
        k2_ref[:, LANE:] = jnp.where(low, 0.0, jnp.where(first, kroll, kk)).astype(BF16)
        v2_ref[...] = jnp.where(low, jnp.where(first, vv, vroll), jnp.where(first, vroll, vv)).astype(BF16)

    qst = jnp.concatenate([q_ref[:, :LANE], q_ref[:, LANE:]], axis=0)
    rows = 4 * tq
    row = qi * tq + lax.broadcasted_iota(jnp.int32, (rows, tk), 0) % tq
    col0 = lax.broadcasted_iota(jnp.int32, (rows, tk), 1)
    umat = _suffix_matrix(tk)
    nk = (qi * tq + tq - 1) // tk + 1

    def body(jj, carry):
        run, acc = carry
        j = nk - 1 - jj
        ks = pl.multiple_of(j * tk, tk)
        kt = k2_ref[pl.ds(ks, tk), :]
        z = jnp.concatenate([_dot_nt(qst, kt[:, :LANE]), _dot_nt(qst, kt[:, LANE:])], axis=0)
        col = col0 + j * tk
        valid = (col >= n_pad) & (col < row)
        l1m = jnp.where(valid, _neg_softplus(z), 0.0)
        hi, lo = _split_hi_lo(l1m)
        tail = _dot(hi, umat) + _dot(lo, umat) + run
        a = jnp.where(valid, jnp.exp(z + l1m + tail), 0.0)
        run = run + jnp.sum(l1m, axis=-1, keepdims=True)
        acc = acc + _dot(a.astype(BF16), v2_ref[pl.ds(ks, tk), :])
        return run, acc

    init = (jnp.zeros((rows, 1), F32), jnp.zeros((rows, LANE), F32))
    _, acc = lax.fori_loop(0, nk, body, init)
    lane = lax.broadcasted_iota(jnp.int32, (tq, LANE), 1)
    o_ref[:, :LANE] = jnp.where(lane < d, acc[0:tq], acc[2 * tq:3 * tq]).astype(o_ref.dtype)
    o_ref[:, LANE:] = jnp.where(lane < d, acc[tq:2 * tq], acc[3 * tq:]).astype(o_ref.dtype)


def _sb_prompt(sq, sk, sv, bsz, lp, n_pad):
    tq = LANE
    tk = ROW_TILE
    nq = lp // tq
    qw = SB_GROUP * SB_HEAD_DIM
    kvw = SB_KV_HEADS * SB_HEAD_DIM
    out = pl.pallas_call(
        functools.partial(_sb_prompt_kernel, tq=tq, tk=tk, n_pad=n_pad),
        grid=(bsz, SB_KV_HEADS, nq),
        in_specs=[pl.BlockSpec((None, tq, qw), lambda b, g, i: (b, i, g)),
                  pl.BlockSpec((None, lp, kvw), lambda b, g, i: (b, 0, 0)),
                  pl.BlockSpec((None, lp, kvw), lambda b, g, i: (b, 0, 0))],
        out_specs=pl.BlockSpec((None, tq, qw), lambda b, g, i: (b, i, g)),
        out_shape=jax.ShapeDtypeStruct((bsz, lp, SB_Q_HEADS * SB_HEAD_DIM), BF16),
        scratch_shapes=[pltpu.VMEM((lp, 2 * LANE), BF16), pltpu.VMEM((lp, LANE), BF16)],
        compiler_params=pltpu.CompilerParams(dimension_semantics=("arbitrary", "arbitrary", "arbitrary"),
                                             vmem_limit_bytes=VMEM_LIMIT),
    )(sq.reshape(bsz, lp, -1), sk.reshape(bsz, lp, kvw), sv.reshape(bsz, lp, kvw))
    return out.reshape(bsz * lp, SB_Q_HEADS * SB_HEAD_DIM)


PAGES_PER_STEP = 4


def _page_specs(page_shape, layer, n_pages):
    specs = []
    nd_tail = len(page_shape)
    for i in range(PAGES_PER_STEP):
        def imap(s, c, pt, i=i):
            return (layer, pt[s, n_pages - 1 - (c * PAGES_PER_STEP + i)]) + (0,) * nd_tail
        specs.append(pl.BlockSpec((None, None) + tuple(page_shape), imap))
    return specs


def _mla_sample_kernel(pt_ref, qm_ref, qr_ref, cnew_ref, krnew_ref, wukt_ref, wuv_ref, *rest, n_new):
    del pt_ref
    c_pages = rest[:PAGES_PER_STEP]
    r_pages = rest[PAGES_PER_STEP:2 * PAGES_PER_STEP]
    o_ref, m_ref, l_ref, acc_ref = rest[2 * PAGES_PER_STEP:]
    step = pl.program_id(1)
    rows = MLA_HEADS * n_new

    def tile(c_tile, kr_tile, mask):
        cb = c_tile.astype(BF16)
        t = cb.shape[0]
        kraw_t = _dot_nt(wukt_ref[...], cb)
        ms = jnp.sum((kraw_t * kraw_t).reshape(MLA_HEADS, MLA_NOPE, t), axis=1) * (1.0 / MLA_NOPE)
        rs = lax.rsqrt(ms + EPS)
        s_nope = _dot_nt(qm_ref[...], cb).reshape(MLA_HEADS, n_new, t) * rs[:, None, :]
        s = s_nope.reshape(rows, t) + _dot(qr_ref[...], kr_tile.astype(BF16))
        if mask is not None:
            s = jnp.where(mask, s, NEG_BIG)
        m_old = m_ref[...]
        m_new = jnp.maximum(m_old, jnp.max(s, axis=-1, keepdims=True))
        p = jnp.exp(s - m_new)
        alpha = jnp.exp(m_old - m_new)
        l_ref[...] = alpha * l_ref[...] + jnp.sum(p, axis=-1, keepdims=True)
        acc_ref[...] = alpha * acc_ref[...] + _dot(p.astype(BF16), cb)
        m_ref[...] = m_new

    @pl.when(step == 0)
    def _():
        m_ref[...] = jnp.full_like(m_ref, NEG_BIG)
        l_ref[...] = jnp.zeros_like(l_ref)
        acc_ref[...] = jnp.zeros_like(acc_ref)
        t = cnew_ref.shape[0]
        qtok = lax.broadcasted_iota(jnp.int32, (rows, t), 0) % n_new
        key = lax.broadcasted_iota(jnp.int32, (rows, t), 1)
        tile(cnew_ref[...], krnew_ref[...], key <= qtok)

    for i in range(PAGES_PER_STEP):
        tile(c_pages[i][...], r_pages[i][...], None)

    @pl.when(step == pl.num_programs(1) - 1)
    def _():
        lat = (acc_ref[...] / l_ref[...]).astype(BF16)
        for hd in range(MLA_HEADS):
            o_ref[hd * n_new:(hd + 1) * n_new, :] = _dot(lat[hd * n_new:(hd + 1) * n_new, :], wuv_ref[hd])


def _mla_sample(layer, page_table, qm, qr, cnew, krnew, wukt, wuv, cache_lat, cache_rope):
    n_seq, n_pages = page_table.shape
    page = cache_lat.shape[2]
    n_new = qm.shape[1] // MLA_HEADS
    rows = qm.shape[1]
    steps = n_pages // PAGES_PER_STEP
    seq3 = lambda a: pl.BlockSpec((None,) + a.shape[1:], lambda s, c, pt: (s, 0, 0))
    const = lambda a: pl.BlockSpec(a.shape, lambda s, c, pt, _n=a.ndim: (0,) * _n)
    in_specs = ([seq3(qm), seq3(qr), seq3(cnew), seq3(krnew), const(wukt), const(wuv)]
                + _page_specs((page, KV_LORA), layer, n_pages)
                + _page_specs((MLA_ROPE, page), layer, n_pages))
    grid_spec = pltpu.PrefetchScalarGridSpec(
        num_scalar_prefetch=1,
        grid=(n_seq, steps),
        in_specs=in_specs,
        out_specs=pl.BlockSpec((None, rows, MLA_V), lambda s, c, pt: (s, 0, 0)),
        scratch_shapes=[pltpu.VMEM((rows, 1), F32), pltpu.VMEM((rows, 1), F32), pltpu.VMEM((rows, KV_LORA), F32)],
    )
    return pl.pallas_call(
        functools.partial(_mla_sample_kernel, n_new=n_new),
        grid_spec=grid_spec,
        out_shape=jax.ShapeDtypeStruct((n_seq, rows, MLA_V), F32),
        compiler_params=pltpu.CompilerParams(dimension_semantics=("arbitrary", "arbitrary"),
                                             vmem_limit_bytes=VMEM_LIMIT),
    )(page_table, qm, qr, cnew, krnew, wukt, wuv, *([cache_lat] * PAGES_PER_STEP), *([cache_rope] * PAGES_PER_STEP))


def _sb_sample_kernel(pt_ref, q2_ref, knew_ref, vnew_ref, *rest, n_new):
    del pt_ref
    k_pages = rest[:PAGES_PER_STEP]
    v_pages = rest[PAGES_PER_STEP:2 * PAGES_PER_STEP]
    o_ref, run_ref, acc_ref = rest[2 * PAGES_PER_STEP:]
    step = pl.program_id(1)
    rows = q2_ref.shape[0]

    def tile(k_tile, v_tile, mask):
        t = k_tile.shape[1]
        z = _dot(q2_ref[...], k_tile.astype(BF16))
        l1m = _neg_softplus(z)
        if mask is not None:
            l1m = jnp.where(mask, l1m, 0.0)
        hi, lo = _split_hi_lo(l1m)
        umat = _suffix_matrix(t)
        tail = _dot(hi, umat) + _dot(lo, umat) + run_ref[...]
        a = jnp.exp(z + l1m + tail)
        if mask is not None:
            a = jnp.where(mask, a, 0.0)
        run_ref[...] = run_ref[...] + jnp.sum(l1m, axis=-1, keepdims=True)
        acc_ref[...] = acc_ref[...] + _dot_nt(a.astype(BF16), v_tile.astype(BF16))

    @pl.when(step == 0)
    def _():
        run_ref[...] = jnp.zeros_like(run_ref)
        acc_ref[...] = jnp.zeros_like(acc_ref)
        t = knew_ref.shape[1]
        qtok = lax.broadcasted_iota(jnp.int32, (rows, t), 0) % n_new
        key = lax.broadcasted_iota(jnp.int32, (rows, t), 1)
        tile(knew_ref[...], vnew_ref[...], key < qtok)

    for i in range(PAGES_PER_STEP):
        tile(k_pages[i][...], v_pages[i][...], None)

    @pl.when(step == pl.num_programs(1) - 1)
    def _():
        o_ref[...] = acc_ref[...]


def _sb_sample(layer, page_table, q2, knew, vnew, cache_k, cache_v, n_new):
    n_seq, n_pages = page_table.shape
    page = cache_k.shape[3]
    kvw = SB_KV_HEADS * SB_HEAD_DIM
    rows = q2.shape[1]
    steps = n_pages // PAGES_PER_STEP
    seq3 = lambda a: pl.BlockSpec((None,) + a.shape[1:], lambda s, c, pt: (s, 0, 0))
    in_specs = ([seq3(q2), seq3(knew), seq3(vnew)]
                + _page_specs((kvw, page), layer, n_pages) + _page_specs((kvw, page), layer, n_pages))
    grid_spec = pltpu.PrefetchScalarGridSpec(
        num_scalar_prefetch=1,
        grid=(n_seq, steps),
        in_specs=in_specs,
        out_specs=pl.BlockSpec((None, rows, kvw), lambda s, c, pt: (s, 0, 0)),
        scratch_shapes=[pltpu.VMEM((rows, 1), F32), pltpu.VMEM((rows, kvw), F32)],
    )
    return pl.pallas_call(
        functools.partial(_sb_sample_kernel, n_new=n_new),
        grid_spec=grid_spec,
        out_shape=jax.ShapeDtypeStruct((n_seq, rows, kvw), F32),
        compiler_params=pltpu.CompilerParams(dimension_semantics=("arbitrary", "arbitrary"),
                                             vmem_limit_bytes=VMEM_LIMIT),
    )(page_table, q2, knew, vnew, *([cache_k] * PAGES_PER_STEP), *([cache_v] * PAGES_PER_STEP))


def _sigmoid(x):
    return 1.0 / (1.0 + jnp.exp(-x))


def _mix_kernel(x_ref, om_ref, yb_ref, os_ref, gmix_ref, wg_ref, wpa_ref, wpb_ref, wpc_ref, wo_ref, o_ref):
    x = x_ref[...]
    d = x.shape[1]
    h = _rms(x, gmix_ref[...]).astype(BF16)
    mix = _sigmoid(_dot(h, wg_ref[:, 0:d])) * _dot(om_ref[...], wpa_ref[...])
    mix = mix + _sigmoid(_dot(h, wg_ref[:, d:2 * d])) * _dot(yb_ref[...], wpb_ref[...])
    mix = mix + _sigmoid(_dot(h, wg_ref[:, 2 * d:3 * d])) * _dot(os_ref[...], wpc_ref[...])
    o_ref[...] = x + _dot(mix.astype(BF16), wo_ref[...])


def _mix(x, om, yb, osb, wts):
    m, d = x.shape
    tm = ROW_TILE
    ws = [wts['gmix'], wts['wg'], wts['wpa'], wts['wpb'], wts['wpc'], wts['wo']]
    return pl.pallas_call(
        _mix_kernel,
        grid=(m // tm,),
        in_specs=[_row_spec(tm, d), _row_spec(tm, om.shape[1]), _row_spec(tm, yb.shape[1]),
                  _row_spec(tm, osb.shape[1])] + [_full_spec(a) for a in ws],
        out_specs=_row_spec(tm, d),
        out_shape=jax.ShapeDtypeStruct((m, d), F32),
        compiler_params=pltpu.CompilerParams(dimension_semantics=("arbitrary",), vmem_limit_bytes=VMEM_LIMIT),
    )(x, om, yb, osb, *ws)


def _ffn_kernel(x_ref, g_ref, w1_ref, w2_ref, o_ref, *, tm, rows_per_seq, n_pad):
    x = x_ref[...]
    h = _rms(x, g_ref[...]).astype(BF16)
    a = jnp.maximum(_dot(h, w1_ref[...]), 0.0)
    y = x + _dot((a * a).astype(BF16), w2_ref[...])
    if n_pad:
        row = (pl.program_id(0) * tm) % rows_per_seq + lax.broadcasted_iota(jnp.int32, y.shape, 0)
        y = jnp.where(row >= n_pad, y, 0.0)
    o_ref[...] = y


def _ffn(x, wts, rows_per_seq, n_pad):
    m, d = x.shape
    tm = ROW_TILE
    ws = [wts['gffn'], wts['w1'], wts['w2']]
    return pl.pallas_call(
        functools.partial(_ffn_kernel, tm=tm, rows_per_seq=rows_per_seq, n_pad=n_pad),
        grid=(m // tm,),
        in_specs=[_row_spec(tm, d)] + [_full_spec(a) for a in ws],
        out_specs=_row_spec(tm, d),
        out_shape=jax.ShapeDtypeStruct((m, d), F32),
        compiler_params=pltpu.CompilerParams(dimension_semantics=("arbitrary",), vmem_limit_bytes=VMEM_LIMIT),
    )(x, *ws)


def _head_indicators(widths_and_offsets, n_cols):
    g = np.zeros((n_cols, LANE), np.float32)
    gt = np.zeros((LANE, n_cols), np.float32)
    for col, (start, width) in enumerate(widths_and_offsets):
        g[start:start + width, col] = 1.0 / width
        gt[col, start:start + width] = 1.0
    return jnp.asarray(g, BF16), jnp.asarray(np.concatenate([gt, gt], axis=0), BF16)


def _head_row(vals_per_head):
    return jnp.tile(vals_per_head, MLA_HEADS)[None, :]


def _prep_layer(l, norm_mix, w_in, q_a_norm, w_uq, kv_a_norm, w_uk, w_uv, q_norm_nope, q_norm_rope,
                k_norm_nope, k_norm_rope, conv_w, w_pa, w_pb, w_pc, w_o, norm_ffn, w_ff1, w_ff2):
    half = MLA_ROPE // 2
    d_model = w_in.shape[1]
    offs = np.cumsum((0,) + SPLIT_SIZES)
    win = w_in[l]
    kr_cols = win[:, offs[2]:offs[3]]
    zeros = lambda n: jnp.zeros((d_model, n), F32)
    kr_main = jnp.concatenate([zeros(MLA_NOPE), kr_cols, zeros(HEAD_PAD - MLA_NOPE - MLA_ROPE)], axis=1)
    kr_part = jnp.concatenate([zeros(MLA_NOPE), kr_cols[:, half:], kr_cols[:, :half],
                               zeros(HEAD_PAD - MLA_NOPE - MLA_ROPE)], axis=1)
    wa = jnp.concatenate([win[:, :offs[2]], kr_main, kr_part, win[:, offs[3]:offs[9]]], axis=1).astype(BF16)
    wg = win[:, offs[9]:].astype(BF16)

    uq = w_uq[l].reshape(Q_LORA, MLA_HEADS, MLA_NOPE + MLA_ROPE)
    zq = lambda n: jnp.zeros((Q_LORA, MLA_HEADS, n), F32)
    rope = uq[..., MLA_NOPE:]
    uq_main = jnp.concatenate([uq, zq(HEAD_PAD - MLA_NOPE - MLA_ROPE)], axis=-1)
    uq_part = jnp.concatenate([zq(MLA_NOPE), rope[..., half:], rope[..., :half],
                               zq(HEAD_PAD - MLA_NOPE - MLA_ROPE)], axis=-1)
    wuq = jnp.concatenate([uq_main.reshape(Q_LORA, _QW), uq_part.reshape(Q_LORA, _QW)], axis=1).astype(BF16)

    z_tail = jnp.zeros((HEAD_PAD - MLA_NOPE - MLA_ROPE,), F32)
    z_nope = jnp.zeros((MLA_NOPE,), F32)
    gr = q_norm_rope[l]
    gqm = _head_row(jnp.concatenate([q_norm_nope[l], gr, z_tail])) * MLA_SCALE
    gqp = _head_row(jnp.concatenate([z_nope, gr[half:], gr[:half], z_tail])) * MLA_SCALE
    gkr_v = k_norm_rope[l]
    gkr = jnp.concatenate([z_nope, gkr_v, z_tail])[None, :]
    gkrp = jnp.concatenate([z_nope, gkr_v[half:], gkr_v[:half], z_tail])[None, :]
    gk = _head_row(jnp.concatenate([k_norm_nope[l], jnp.zeros((HEAD_PAD - MLA_NOPE,), F32)]))

    q_groups = ([(h * HEAD_PAD, MLA_NOPE) for h in range(MLA_HEADS)]
                + [(h * HEAD_PAD + MLA_NOPE, MLA_ROPE) for h in range(MLA_HEADS)])
    gq_ind, gqt_ind = _head_indicators(q_groups, _QW)
    gk_ind, gkt_ind = _head_indicators([(h * HEAD_PAD, MLA_NOPE) for h in range(MLA_HEADS)], _QW)

    uk = w_uk[l]
    wuk = jnp.concatenate([uk, jnp.zeros((KV_LORA, MLA_HEADS, HEAD_PAD - MLA_NOPE), F32)], axis=-1)
    wuk = wuk.reshape(KV_LORA, _QW).astype(BF16)
    wukt = jnp.transpose(uk, (1, 2, 0)).reshape(MLA_HEADS * MLA_NOPE, KV_LORA).astype(BF16)
    ukt_pad = jnp.concatenate([jnp.transpose(uk, (1, 2, 0)),
                               jnp.zeros((MLA_HEADS, HEAD_PAD - MLA_NOPE, KV_LORA), F32)], axis=1)
    eye = jnp.eye(MLA_HEADS, dtype=F32)
    wabs = (eye[:, None, :, None] * ukt_pad[:, :, None, :]).reshape(_QW, MLA_HEADS * KV_LORA).astype(BF16)
    wuv_flat = w_uv[l].reshape(KV_LORA, MLA_HEADS * MLA_V).astype(BF16)
    wuv_heads = jnp.transpose(w_uv[l], (1, 0, 2)).astype(BF16)

    return dict(
        gmix=norm_mix[l][None, :], wa=wa, wg=wg, gqa=q_a_norm[l][None, :], wuq=wuq,
        gq_ind=gq_ind, gqt_ind=gqt_ind, gqm=gqm, gqp=gqp, gkva=kv_a_norm[l][None, :], gkr=gkr, gkrp=gkrp,
        convw=jnp.concatenate([conv_w[l], jnp.zeros((SUBLANE - CONV_K, CONV_DIM), F32)], axis=0),
        gk=gk, wuk=wuk, gk_ind=gk_ind, gkt_ind=gkt_ind, wuv=wuv_flat, wukt=wukt, wabs=wabs, wuv_heads=wuv_heads,
        wpa=w_pa[l].astype(BF16), wpb=w_pb[l].astype(BF16), wpc=w_pc[l].astype(BF16), wo=w_o[l].astype(BF16),
        gffn=norm_ffn[l][None, :], w1=w_ff1[l].astype(BF16), w2=w_ff2[l].astype(BF16))


def _rope_tables(pos):
    inv = ROPE_BASE ** (-jnp.arange(0, MLA_ROPE, 2, dtype=F32) / MLA_ROPE)
    ang = pos.astype(F32)[:, None] * inv[None, :]
    cos, sin = jnp.cos(ang), jnp.sin(ang)
    t = pos.shape[0]
    tail = jnp.zeros((t, HEAD_PAD - MLA_NOPE - MLA_ROPE), F32)
    cos_t = jnp.concatenate([jnp.ones((t, MLA_NOPE), F32), cos, cos, tail], axis=1)
    sin_t = jnp.concatenate([jnp.zeros((t, MLA_NOPE), F32), -sin, sin, tail], axis=1)
    return cos_t, sin_t


def kernel(x_prompt, x_sample, cache_mla_latent, cache_mla_rope, cache_sb_k, cache_sb_v, state_conv, page_table, meta_tokens, norm_mix, w_in, q_a_norm, w_uq, kv_a_norm, w_uk, w_uv, q_norm_nope, q_norm_rope, k_norm_nope, k_norm_rope, conv_w, w_pa, w_pb, w_pc, w_o, norm_ffn, w_ff1, w_ff2):
    bsz, seq, d_model = x_prompt.shape
    n_seq, n_new, _ = x_sample.shape
    depth = w_in.shape[0]
    n_pages = page_table.shape[1]
    page = cache_mla_latent.shape[2]
    past_len = n_pages * page
    assert n_new == SUBLANE and page == LANE
    l_real = seq + N_META
    lp = -(-l_real // ROW_TILE) * ROW_TILE
    n_pad = lp - l_real
    kvw = SB_KV_HEADS * SB_HEAD_DIM

    meta = jnp.broadcast_to(meta_tokens[None].astype(x_prompt.dtype), (bsz, N_META, d_model))
    xp = jnp.concatenate([jnp.zeros((bsz, n_pad, d_model), x_prompt.dtype), meta, x_prompt], axis=1)
    xp = xp.reshape(bsz * lp, d_model)
    xs = x_sample.reshape(n_seq * n_new, d_model)

    cos_p, sin_p = _rope_tables(jnp.arange(lp) - n_pad)
    cos_p = jnp.tile(cos_p, (bsz, 1))
    sin_p = jnp.tile(sin_p, (bsz, 1))
    cos_s, sin_s = _rope_tables(past_len + jnp.arange(n_new))
    cos_s = jnp.tile(cos_s, (n_seq, 1))
    sin_s = jnp.tile(sin_s, (n_seq, 1))

    cache_kt = jnp.transpose(cache_sb_k, (0, 1, 3, 4, 2)).reshape(cache_sb_k.shape[:2] + (kvw, page))
    cache_vt = jnp.transpose(cache_sb_v, (0, 1, 3, 4, 2)).reshape(cache_sb_v.shape[:2] + (kvw, page))
    cache_rope_t = jnp.transpose(cache_mla_rope, (0, 1, 3, 2))

    outs = {k: [] for k in ('p_lat', 'p_rope', 'p_k', 'p_v', 'p_conv', 's_lat', 's_rope', 's_k', 's_v', 's_conv')}
    for l in range(depth):
        wts = _prep_layer(l, norm_mix, w_in, q_a_norm, w_uq, kv_a_norm, w_uk, w_uv, q_norm_nope, q_norm_rope,
                          k_norm_nope, k_norm_rope, conv_w, w_pa, w_pb, w_pc, w_o, norm_ffn, w_ff1, w_ff2)

        q, k, v, c, kr, sk, sv, sq, yb, utail = _premix(xp, cos_p, sin_p, wts, sample=False)
        o_m = _mla_prompt(q, k, v, bsz, lp, n_pad)
        o_s = _sb_prompt(sq, sk, sv, bsz, lp, n_pad)
        xp = _ffn(_mix(xp, o_m, yb, o_s, wts), wts, lp, n_pad)
        outs['p_lat'].append(c.reshape(bsz, lp, KV_LORA)[:, n_pad:])
        outs['p_rope'].append(kr.reshape(bsz, lp, HEAD_PAD)[:, n_pad:, MLA_NOPE:MLA_NOPE + MLA_ROPE])
        outs['p_k'].append(sk.reshape(bsz, lp, SB_KV_HEADS, SB_HEAD_DIM)[:, n_pad:])
        outs['p_v'].append(sv.reshape(bsz, lp, SB_KV_HEADS, SB_HEAD_DIM)[:, n_pad:])
        outs['p_conv'].append(utail.reshape(bsz, lp // ROW_TILE, SUBLANE, CONV_DIM)[:, -1, SUBLANE - (CONV_K - 1):])

        st = state_conv[l]
        p2 = jnp.pad(st, ((0, 0), (0, n_new - (CONV_K - 1)), (0, 0))).reshape(n_seq * n_new, CONV_DIM)
        p1 = jnp.pad(st[:, 1:], ((0, 0), (0, n_new - 1), (0, 0))).reshape(n_seq * n_new, CONV_DIM)
        q, qabs, c, kr, sk, sv, sq, yb, u = _premix(xs, cos_s, sin_s, wts, sample=True, p1=p1, p2=p2)
        rows = MLA_HEADS * n_new
        qm = jnp.transpose(qabs.reshape(n_seq, n_new, MLA_HEADS, KV_LORA), (0, 2, 1, 3))
        qm = qm.reshape(n_seq, rows, KV_LORA).astype(BF16)
        qr = q.reshape(n_seq, n_new, MLA_HEADS, HEAD_PAD)[..., MLA_NOPE:MLA_NOPE + MLA_ROPE]
        qr = jnp.transpose(qr, (0, 2, 1, 3)).reshape(n_seq, rows, MLA_ROPE).astype(BF16)
        pad_new = lambda a: jnp.pad(a.reshape(n_seq, n_new, -1), ((0, 0), (0, page - n_new), (0, 0)))
        pad_new_t = lambda a: jnp.transpose(pad_new(a), (0, 2, 1))
        kr32 = kr[:, MLA_NOPE:MLA_NOPE + MLA_ROPE]
        o_m = _mla_sample(l, page_table, qm, qr, pad_new(c), pad_new_t(kr32), wts['wukt'], wts['wuv_heads'],
                          cache_mla_latent, cache_rope_t)
        o_m = jnp.transpose(o_m.reshape(n_seq, MLA_HEADS, n_new, MLA_V), (0, 2, 1, 3))
        o_m = o_m.reshape(n_seq * n_new, MLA_HEADS * MLA_V).astype(BF16)

        sq5 = jnp.transpose(sq.reshape(n_seq, n_new, SB_KV_HEADS, SB_GROUP, SB_HEAD_DIM), (0, 2, 3, 1, 4))
        zq = jnp.zeros_like(sq5[:, 0])
        q2 = jnp.stack([jnp.concatenate([sq5[:, 0], zq], axis=-1), jnp.concatenate([zq, sq5[:, 1]], axis=-1)], axis=1)
        q2 = q2.reshape(n_seq, SB_Q_HEADS * n_new, kvw).astype(BF16)
        o_s = _sb_sample(l, page_table, q2, pad_new_t(sk), pad_new_t(sv), cache_kt, cache_vt, n_new)
        o_s = o_s.reshape(n_seq, SB_KV_HEADS, SB_GROUP, n_new, SB_KV_HEADS, SB_HEAD_DIM)
        o_s = jnp.stack([o_s[:, 0, :, :, 0], o_s[:, 1, :, :, 1]], axis=1)
        o_s = jnp.transpose(o_s, (0, 3, 1, 2, 4)).reshape(n_seq * n_new, SB_Q_HEADS * SB_HEAD_DIM).astype(BF16)
        xs = _ffn(_mix(xs, o_m, yb, o_s, wts), wts, n_new, 0)
        outs['s_lat'].append(c.reshape(n_seq, n_new, KV_LORA))
        outs['s_rope'].append(kr32.reshape(n_seq, n_new, MLA_ROPE))
        outs['s_k'].append(sk.reshape(n_seq, n_new, SB_KV_HEADS, SB_HEAD_DIM))
        outs['s_v'].append(sv.reshape(n_seq, n_new, SB_KV_HEADS, SB_HEAD_DIM))
        outs['s_conv'].append(u.reshape(n_seq, n_new, CONV_DIM)[:, n_new - (CONV_K - 1):])

    y_prompt = xp.reshape(bsz, lp, d_model)[:, n_pad + N_META:]
    y_sample = xs.reshape(n_seq, n_new, d_model)
    st = lambda name: jnp.stack(outs[name])
    return (y_prompt, y_sample, st('p_lat'), st('p_rope'), st('p_k'), st('p_v'), st('p_conv'),
            st('s_lat'), st('s_rope'), st('s_k'), st('s_v'), st('s_conv'))
```

```python
import functools

import numpy as np
import jax
import jax.numpy as jnp
from jax import lax
from jax.experimental import pallas as pl
from jax.experimental.pallas import tpu as pltpu

N_META = 16
EPS = 1e-6
MLA_HEADS = 8
MLA_NOPE = 64
MLA_ROPE = 32
MLA_V = 64
Q_LORA = 384
KV_LORA = 256
ROPE_BASE = 10000.0
MLA_SCALE = (MLA_NOPE + MLA_ROPE) ** -0.5
CONV_DIM = 512
CONV_K = 3
SB_Q_HEADS = 8
SB_KV_HEADS = 2
SB_GROUP = SB_Q_HEADS // SB_KV_HEADS
SB_HEAD_DIM = 64
SB_SCALE = SB_HEAD_DIM ** -0.5
SPLIT_SIZES = (Q_LORA, KV_LORA, MLA_ROPE, CONV_DIM, CONV_DIM, CONV_DIM,
               SB_Q_HEADS * SB_HEAD_DIM, SB_KV_HEADS * SB_HEAD_DIM, SB_KV_HEADS * SB_HEAD_DIM)

LANE = 128
SUBLANE = 8
HEAD_PAD = LANE
ROW_TILE = 256
NEG_BIG = -1e30
BIAS_LANE = MLA_NOPE + MLA_ROPE
PAD_KEY_BIAS = -30000.0
MLA_PROMPT_HEADS_PER_STEP = 4
VMEM_LIMIT = 56 * 1024 * 1024

F32 = jnp.float32
BF16 = jnp.bfloat16

_A_QLAT = 0
_A_C = _A_QLAT + Q_LORA
_A_KR = _A_C + KV_LORA
_A_KRP = _A_KR + HEAD_PAD
_A_B = _A_KRP + HEAD_PAD
_A_CI = _A_B + CONV_DIM
_A_X = _A_CI + CONV_DIM
_A_SQ = _A_X + CONV_DIM
_A_SK = _A_SQ + SB_Q_HEADS * SB_HEAD_DIM
_A_SV = _A_SK + SB_KV_HEADS * SB_HEAD_DIM
_A_END = _A_SV + SB_KV_HEADS * SB_HEAD_DIM
_QW = MLA_HEADS * HEAD_PAD


def _dot(a, b):
    return jnp.dot(a, b, preferred_element_type=F32)


def _dot_nt(a, b):
    return lax.dot_general(a, b, (((1,), (1,)), ((), ())), preferred_element_type=F32)


def _rms(x, g):
    return x * lax.rsqrt(jnp.mean(x * x, axis=-1, keepdims=True) + EPS) * g


def _split_hi_lo(x):
    hi = x.astype(BF16)
    lo = (x - hi.astype(F32)).astype(BF16)
    return hi, lo


def _group_rs(raw, g_ref, gt2_ref):
    ms = _dot((raw * raw).astype(BF16), g_ref[...])
    rs = lax.rsqrt(ms + EPS)
    hi, lo = _split_hi_lo(rs)
    return _dot(jnp.concatenate([hi, lo], axis=1), gt2_ref[...])


def _premix_kernel(*refs, sample, tm, rows_per_seq, n_pad):
    if sample:
        (x_ref, cos_ref, sin_ref, p1_ref, p2_ref, gmix_ref, wa_ref, gqa_ref, wuq_ref, gq_ind, gqt_ind,
         gqm_ref, gqp_ref, gkva_ref, gkr_ref, gkrp_ref, convw_ref, gk_ref, wabs_ref,
         q_out, qabs_out, c_out, kr_out, sk_out, sv_out, sq_out, yb_out, u_out) = refs
    else:
        (x_ref, cos_ref, sin_ref, gmix_ref, wa_ref, gqa_ref, wuq_ref, gq_ind, gqt_ind,
         gqm_ref, gqp_ref, gkva_ref, gkr_ref, gkrp_ref, convw_ref, gk_ref, wuk_ref, gk_ind, gkt_ind, wuv_ref,
         vone_ref, q_out, k_out, v_out, c_out, kr_out, sk_out, sv_out, sq_out, yb_out, utail_out, carry_ref) = refs

    x = x_ref[...]
    h = _rms(x, gmix_ref[...]).astype(BF16)

    def proj(a, b):
        return _dot(h, wa_ref[:, a:b])

    cos = cos_ref[...]
    sin = sin_ref[...]

    ql = _rms(proj(_A_QLAT, _A_C), gqa_ref[...]).astype(BF16)
    qraw = _dot(ql, wuq_ref[...])
    qm = qraw[:, :_QW]
    qp = qraw[:, _QW:]
    rs_full = _group_rs(qm, gq_ind, gqt_ind)
    ym = qm * rs_full * gqm_ref[...]
    yp = qp * rs_full * gqp_ref[...]
    q_heads = []
    bias_lane = lax.broadcasted_iota(jnp.int32, (tm, HEAD_PAD), 1) == BIAS_LANE
    for hd in range(MLA_HEADS):
        sl = slice(hd * HEAD_PAD, (hd + 1) * HEAD_PAD)
        q_h = ym[:, sl] * cos + yp[:, sl] * sin
        if not sample:
            q_h = jnp.where(bias_lane, 1.0, q_h)
        q_heads.append(q_h)
    q_full = jnp.concatenate(q_heads, axis=1)
    q_out[...] = q_full.astype(q_out.dtype)

    c = _rms(proj(_A_C, _A_KR), gkva_ref[...])
    c_out[...] = c
    c_bf = c.astype(BF16)
    kr_raw = proj(_A_KR, _A_KRP)
    kr_par = proj(_A_KRP, _A_B)
    kr_rs = lax.rsqrt(jnp.sum(kr_raw * kr_raw, axis=-1, keepdims=True) * (1.0 / MLA_ROPE) + EPS)
    kr128 = (kr_raw * kr_rs * gkr_ref[...]) * cos + (kr_par * kr_rs * gkrp_ref[...]) * sin
    kr_out[...] = kr128

    if sample:
        qabs_out[...] = _dot((q_full * gk_ref[...]).astype(BF16), wabs_ref[...])
    else:
        kraw = _dot(c_bf, wuk_ref[...])
        kn = kraw * _group_rs(kraw, gk_ind, gkt_ind) * gk_ref[...]
        row_in_seq = (pl.program_id(0) * tm) % rows_per_seq + lax.broadcasted_iota(jnp.int32, (tm, HEAD_PAD), 0)
        k_tail = kr128 + jnp.where(bias_lane & (row_in_seq < n_pad), PAD_KEY_BIAS, 0.0)
        for hd in range(MLA_HEADS):
            sl = slice(hd * HEAD_PAD, (hd + 1) * HEAD_PAD)
            k_out[:, sl] = (kn[:, sl] + k_tail).astype(k_out.dtype)
        v_out[...] = (_dot(c_bf, wuv_ref[...]) + vone_ref[...]).astype(v_out.dtype)

    sq_out[...] = (proj(_A_SQ, _A_SK) * SB_SCALE).astype(sq_out.dtype)
    sk_out[...] = proj(_A_SK, _A_SV)
    sv_out[...] = proj(_A_SV, _A_END)

    u = proj(_A_CI, _A_X) * proj(_A_X, _A_SQ)
    r1 = pltpu.roll(u, 1, 0)
    r2 = pltpu.roll(u, 2, 0)
    row = lax.broadcasted_iota(jnp.int32, u.shape, 0)
    if sample:
        in_seq = row % SUBLANE
        prev1 = jnp.where(in_seq == 0, p1_ref[...], r1)
        prev2 = jnp.where(in_seq < 2, p2_ref[...], r2)
        u_out[...] = u
    else:
        @pl.when(pl.program_id(0) == 0)
        def _():
            carry_ref[...] = jnp.zeros_like(carry_ref)
        last1 = carry_ref[SUBLANE - 1:SUBLANE, :]
        last2 = carry_ref[SUBLANE - 2:SUBLANE - 1, :]
        prev1 = jnp.where(row == 0, last1, r1)
        prev2 = jnp.where(row == 0, last2, jnp.where(row == 1, last1, r2))
        tail = u[tm - SUBLANE:, :]
        carry_ref[...] = tail
        utail_out[...] = tail
    w = convw_ref[...]
    y_conv = w[0:1, :] * prev2 + w[1:2, :] * prev1 + w[2:3, :] * u
    yb_out[...] = (proj(_A_B, _A_CI) * y_conv).astype(yb_out.dtype)


def _full_spec(a):
    nd = a.ndim
    return pl.BlockSpec(a.shape, lambda i, _nd=nd: (0,) * _nd)


def _row_spec(tm, width):
    return pl.BlockSpec((tm, width), lambda i: (i, 0))


def _premix(x, cos, sin, wts, sample, rows_per_seq, n_pad, p1=None, p2=None):
    m = x.shape[0]
    tm = ROW_TILE
    d_model = x.shape[1]
    nblk = m // tm
    row = lambda w_: _row_spec(tm, w_)
    common_w = [wts['gmix'], wts['wa'], wts['gqa'], wts['wuq'], wts['gq_ind'], wts['gqt_ind'],
                wts['gqm'], wts['gqp'], wts['gkva'], wts['gkr'], wts['gkrp'], wts['convw'], wts['gk']]
    if sample:
        ins = [x, cos, sin, p1, p2] + common_w + [wts['wabs']]
        in_specs = ([row(d_model), row(HEAD_PAD), row(HEAD_PAD), row(CONV_DIM), row(CONV_DIM)]
                    + [_full_spec(a) for a in common_w + [wts['wabs']]])
        outs = [(_QW, F32), (MLA_HEADS * KV_LORA, F32), (KV_LORA, F32), (HEAD_PAD, F32),
                (SB_KV_HEADS * SB_HEAD_DIM, F32), (SB_KV_HEADS * SB_HEAD_DIM, F32),
                (SB_Q_HEADS * SB_HEAD_DIM, F32), (CONV_DIM, BF16), (CONV_DIM, F32)]
        out_shape = [jax.ShapeDtypeStruct((m, w_), dt) for w_, dt in outs]
        out_specs = [row(w_) for w_, _ in outs]
        scratch = []
    else:
        extra = [wts['wuk'], wts['gk_ind'], wts['gkt_ind'], wts['wuv'], wts['vone']]
        ins = [x, cos, sin] + common_w + extra
        in_specs = ([row(d_model), row(HEAD_PAD), row(HEAD_PAD)] + [_full_spec(a) for a in common_w + extra])
        outs = [(_QW, BF16), (_QW, BF16), (_QW, BF16), (KV_LORA, F32), (HEAD_PAD, F32),
                (SB_KV_HEADS * SB_HEAD_DIM, F32), (SB_KV_HEADS * SB_HEAD_DIM, F32),
                (SB_Q_HEADS * SB_HEAD_DIM, BF16), (CONV_DIM, BF16)]
        out_shape = [jax.ShapeDtypeStruct((m, w_), dt) for w_, dt in outs]
        out_shape.append(jax.ShapeDtypeStruct((nblk * SUBLANE, CONV_DIM), F32))
        out_specs = [row(w_) for w_, _ in outs] + [pl.BlockSpec((SUBLANE, CONV_DIM), lambda i: (i, 0))]
        scratch = [pltpu.VMEM((SUBLANE, CONV_DIM), F32)]
    return pl.pallas_call(
        functools.partial(_premix_kernel, sample=sample, tm=tm, rows_per_seq=rows_per_seq, n_pad=n_pad),
        grid=(nblk,),
        in_specs=in_specs,
        out_specs=out_specs,
        out_shape=out_shape,
        scratch_shapes=scratch,
        compiler_params=pltpu.CompilerParams(dimension_semantics=("arbitrary",),
                                             vmem_limit_bytes=VMEM_LIMIT),
    )(*ins)


def _mla_prompt_kernel(q_ref, k_ref, v_ref, o_ref, *, tq, tk, heads):
    qi = pl.program_id(2)
    qs = [q_ref[:, hh * HEAD_PAD:(hh + 1) * HEAD_PAD] for hh in range(heads)]
    causal = lax.broadcasted_iota(jnp.int32, (tq, tk), 1) <= lax.broadcasted_iota(jnp.int32, (tq, tk), 0)

    def tile(j, carry, diagonal):
        ks = pl.multiple_of(j * tk, tk)
        sls = [slice(hh * HEAD_PAD, (hh + 1) * HEAD_PAD) for hh in range(heads)]
        ss = [_dot_nt(qs[hh], k_ref[pl.ds(ks, tk), sls[hh]]) for hh in range(heads)]
        ps, scaled, m_news = [], [], []
        for hh in range(heads):
            m, acc = carry[hh]
            s = jnp.where(causal, ss[hh], NEG_BIG) if diagonal else ss[hh]
            m_new = jnp.maximum(m, jnp.max(s, axis=-1, keepdims=True))
            ps.append(jnp.exp(s - m_new).astype(BF16))
            scaled.append(jnp.exp(m - m_new) * acc)
            m_news.append(m_new)
        return tuple((m_news[hh], scaled[hh] + _dot(ps[hh], v_ref[pl.ds(ks, tk), sls[hh]])) for hh in range(heads))

    init = tuple((jnp.full((tq, 1), NEG_BIG, F32), jnp.zeros((tq, HEAD_PAD), F32)) for _ in range(heads))
    carry = lax.fori_loop(0, qi, lambda j, c: tile(j, c, False), init)
    carry = tile(qi, carry, True)
    lane = lax.broadcasted_iota(jnp.int32, (tq, HEAD_PAD), 1)
    outs = [acc / pltpu.roll(acc, MLA_V, 1) for _, acc in carry]
    for pair in range(heads // 2):
        both = jnp.where(lane < MLA_V, outs[2 * pair], pltpu.roll(outs[2 * pair + 1], MLA_V, 1))
        o_ref[:, pair * HEAD_PAD:(pair + 1) * HEAD_PAD] = both.astype(o_ref.dtype)


def _mla_prompt(q, k, v, bsz, lp):
    tq = tk = ROW_TILE
    nq = lp // tq
    heads = MLA_PROMPT_HEADS_PER_STEP
    q3 = q.reshape(bsz, lp, _QW)
    k3 = k.reshape(bsz, lp, _QW)
    v3 = v.reshape(bsz, lp, _QW)
    out = pl.pallas_call(
        functools.partial(_mla_prompt_kernel, tq=tq, tk=tk, heads=heads),
        grid=(bsz, MLA_HEADS // heads, nq),
        in_specs=[pl.BlockSpec((None, tq, heads * HEAD_PAD), lambda b, hp, i: (b, i, hp)),
                  pl.BlockSpec((None, lp, heads * HEAD_PAD), lambda b, hp, i: (b, 0, hp)),
                  pl.BlockSpec((None, lp, heads * HEAD_PAD), lambda b, hp, i: (b, 0, hp))],
        out_specs=pl.BlockSpec((None, tq, heads * MLA_V), lambda b, hp, i: (b, i, hp)),
        out_shape=jax.ShapeDtypeStruct((bsz, lp, MLA_HEADS * MLA_V), BF16),
        compiler_params=pltpu.CompilerParams(dimension_semantics=("arbitrary", "arbitrary", "arbitrary"),
                                             vmem_limit_bytes=VMEM_LIMIT),
    )(q3, k3, v3)
    return out.reshape(bsz * lp, MLA_HEADS * MLA_V)


def _suffix_matrix2(t):
    u = np.triu(np.ones((t, t), np.float32), 1).T
    return jnp.asarray(np.concatenate([u, u], axis=0), BF16)


def _sb_prompt_kernel(q_ref, k_ref, v_ref, u_ref, o_ref, k2_ref, v2_ref, *, tq, tk, n_pad):
    kvh = pl.program_id(1)
    qi = pl.program_id(2)
    d = SB_HEAD_DIM

    @pl.when(qi == 0)
    def _():
        kk = k_ref[...]
        vv = v_ref[...]
        kroll = pltpu.roll(kk, d, 1)
        vroll = pltpu.roll(vv, d, 1)
        lane = lax.broadcasted_iota(jnp.int32, kk.shape, 1)
        first = kvh == 0
        low = lane < d
        k2_ref[:, :LANE] = jnp.where(low, jnp.where(first, kk, kroll), 0.0).astype(BF16)
        k2_ref[:, LANE:] = jnp.where(low, 0.0, jnp.where(first, kroll, kk)).astype(BF16)
        v2_ref[...] = jnp.where(low, jnp.where(first, vv, vroll), jnp.where(first, vroll, vv)).astype(BF16)

    qst = jnp.concatenate([q_ref[:, :LANE], q_ref[:, LANE:]], axis=0)
    rows = 4 * tq
    row = qi * tq + lax.broadcasted_iota(jnp.int32, (rows, tk), 0) % tq
    col0 = lax.broadcasted_iota(jnp.int32, (rows, tk), 1)
    nk = (qi * tq + tq - 1) // tk + 1

    def tiles(js, carry, masked):
        run, acc = carry
        starts = [pl.multiple_of(j * tk, tk) for j in js]
        zs, valids = [], []
        for j, ks in zip(js, starts):
            kt = k2_ref[pl.ds(ks, tk), :]
            zs.append(jnp.concatenate([_dot_nt(qst, kt[:, :LANE]), _dot_nt(qst, kt[:, LANE:])], axis=0))
            col = col0 + j * tk
            valids.append((col >= n_pad) & (col < row) if masked else None)
        sps, l1ms, splits = [], [], []
        for z, valid in zip(zs, valids):
            nz = -z
            sp = jnp.log(1.0 + jnp.exp(jnp.minimum(z, nz)))
            l1m = jnp.minimum(nz, 0.0) - sp
            if valid is not None:
                l1m = jnp.where(valid, l1m, 0.0)
            hi, lo = _split_hi_lo(l1m)
            splits.append(jnp.concatenate([hi, lo], axis=1))
            sps.append(sp)
            l1ms.append(l1m)
        sufs = [_dot(hl, u_ref[...]) for hl in splits]
        weights = []
        for z, valid, sp, l1m, suf in zip(zs, valids, sps, l1ms, sufs):
            a = jnp.exp(jnp.minimum(z, 0.0) - sp + suf + run)
            if valid is not None:
                a = jnp.where(valid, a, 0.0)
            run = run + jnp.sum(l1m, axis=-1, keepdims=True)
            weights.append(a.astype(BF16))
        for ks, a in zip(starts, weights):
            acc = acc + _dot(a, v2_ref[pl.ds(ks, tk), :])
        return run, acc

    n_int = jnp.maximum(nk - 2, 0)
    same = lambda c: c
    carry = (jnp.zeros((rows, 1), F32), jnp.zeros((rows, LANE), F32))
    carry = tiles([nk - 1], carry, True)
    carry = lax.fori_loop(0, n_int // 2, lambda jj, c: tiles([nk - 2 - 2 * jj, nk - 3 - 2 * jj], c, False), carry)
    carry = lax.cond(n_int % 2 == 1, lambda c: tiles([1], c, False), same, carry)
    carry = lax.cond(nk >= 2, lambda c: tiles([0], c, True), same, carry)
    _, acc = carry
    lane = lax.broadcasted_iota(jnp.int32, (tq, LANE), 1)
    o_ref[:, :LANE] = jnp.where(lane < d, acc[0:tq], acc[2 * tq:3 * tq]).astype(o_ref.dtype)
    o_ref[:, LANE:] = jnp.where(lane < d, acc[tq:2 * tq], acc[3 * tq:]).astype(o_ref.dtype)


def _sb_prompt(sq, sk, sv, bsz, lp, n_pad):
    tq = LANE
    tk = ROW_TILE
    nq = lp // tq
    qw = SB_GROUP * SB_HEAD_DIM
    kvw = SB_KV_HEADS * SB_HEAD_DIM
    umat2 = _suffix_matrix2(tk)
    out = pl.pallas_call(
        functools.partial(_sb_prompt_kernel, tq=tq, tk=tk, n_pad=n_pad),
        grid=(bsz, SB_KV_HEADS, nq),
        in_specs=[pl.BlockSpec((None, tq, qw), lambda b, g, i: (b, i, g)),
                  pl.BlockSpec((None, lp, kvw), lambda b, g, i: (b, 0, 0)),
                  pl.BlockSpec((None, lp, kvw), lambda b, g, i: (b, 0, 0)),
                  pl.BlockSpec(umat2.shape, lambda b, g, i: (0, 0))],
        out_specs=pl.BlockSpec((None, tq, qw), lambda b, g, i: (b, i, g)),
        out_shape=jax.ShapeDtypeStruct((bsz, lp, SB_Q_HEADS * SB_HEAD_DIM), BF16),
        scratch_shapes=[pltpu.VMEM((lp, 2 * LANE), BF16), pltpu.VMEM((lp, LANE), BF16)],
        compiler_params=pltpu.CompilerParams(dimension_semantics=("arbitrary", "arbitrary", "arbitrary"),
                                             vmem_limit_bytes=VMEM_LIMIT),
    )(sq.reshape(bsz, lp, -1), sk.reshape(bsz, lp, kvw), sv.reshape(bsz, lp, kvw), umat2)
    return out.reshape(bsz * lp, SB_Q_HEADS * SB_HEAD_DIM)


PAGES_PER_STEP = 16


def _pages_per_step(n_pages):
    g = min(PAGES_PER_STEP, n_pages)
    assert g % 2 == 0 and n_pages % g == 0
    return g


def _page_specs(page_shape, layer, n_pages, g):
    specs = []
    nd_tail = len(page_shape)
    for i in range(g):
        def imap(s, c, pt, i=i):
            return (layer, pt[s, n_pages - 1 - (c * g + i)]) + (0,) * nd_tail
        specs.append(pl.BlockSpec((None, None) + tuple(page_shape), imap))
    return specs


def _mla_sample_kernel(pt_ref, qm_ref, qr_ref, cnew_ref, krnew_ref, wukt_ref, wuv_ref, *rest, n_new, g):
    del pt_ref
    c_pages = rest[:g]
    r_pages = rest[g:2 * g]
    o_ref, m_ref, l_ref, acc_ref, w_ref, s_ref, cb_ref = rest[2 * g:]
    step = pl.program_id(1)
    rows = MLA_HEADS * n_new
    nk = MLA_HEADS * MLA_NOPE

    def scores(cb, kr_t):
        t = cb.shape[0]
        big = _dot_nt(w_ref[...], cb)
        kraw_t = big[:nk]
        ms = jnp.sum((kraw_t * kraw_t).reshape(MLA_HEADS, MLA_NOPE, t), axis=1) * (1.0 / MLA_NOPE)
        rs = lax.rsqrt(ms + EPS)
        s_nope = big[nk:].reshape(MLA_HEADS, n_new, t) * rs[:, None, :]
        return s_nope.reshape(rows, t) + _dot(qr_ref[...], kr_t.astype(BF16))

    def update(s, cbs):
        m_old = m_ref[...]
        m_new = jnp.maximum(m_old, jnp.max(s, axis=-1, keepdims=True))
        p = jnp.exp(s - m_new)
        alpha = jnp.exp(m_old - m_new)
        l_ref[...] = alpha * l_ref[...] + jnp.sum(p, axis=-1, keepdims=True)
        pb = p.astype(BF16)
        acc = alpha * acc_ref[...]
        off = 0
        for cb in cbs:
            t = cb.shape[0]
            acc = acc + _dot(pb[:, off:off + t], cb)
            off += t
        acc_ref[...] = acc
        m_ref[...] = m_new

    @pl.when(step == 0)
    def _():
        m_ref[...] = jnp.full_like(m_ref, NEG_BIG)
        l_ref[...] = jnp.zeros_like(l_ref)
        acc_ref[...] = jnp.zeros_like(acc_ref)
        w_ref[:nk, :] = wukt_ref[...]
        w_ref[nk:, :] = qm_ref[...]
        t = cnew_ref.shape[0]
        qtok = lax.broadcasted_iota(jnp.int32, (rows, t), 0) % n_new
        key = lax.broadcasted_iota(jnp.int32, (rows, t), 1)
        cb = cnew_ref[...].astype(BF16)
        update(jnp.where(key <= qtok, scores(cb, krnew_ref[...]), NEG_BIG), [cb])

    t2 = 2 * c_pages[0].shape[0]
    for pair in range(g // 2):
        newer, older = 2 * pair, 2 * pair + 1
        cb = jnp.concatenate([c_pages[older][...].astype(BF16), c_pages[newer][...].astype(BF16)], axis=0)
        kr = jnp.concatenate([r_pages[older][...], r_pages[newer][...]], axis=1)
        s_ref[:, pair * t2:(pair + 1) * t2] = scores(cb, kr)
        cb_ref[pair] = cb
    update(s_ref[...], [cb_ref[pair] for pair in range(g // 2)])

    @pl.when(step == pl.num_programs(1) - 1)
    def _():
        lat = (acc_ref[...] / l_ref[...]).astype(BF16)
        for hd in range(MLA_HEADS):
            o_ref[hd * n_new:(hd + 1) * n_new, :] = _dot(lat[hd * n_new:(hd + 1) * n_new, :], wuv_ref[hd])


def _mla_sample(layer, page_table, qm, qr, cnew, krnew, wukt, wuv, cache_lat, cache_rope):
    n_seq, n_pages = page_table.shape
    page = cache_lat.shape[2]
    n_new = qm.shape[1] // MLA_HEADS
    rows = qm.shape[1]
    g = _pages_per_step(n_pages)
    steps = n_pages // g
    seq3 = lambda a: pl.BlockSpec((None,) + a.shape[1:], lambda s, c, pt: (s, 0, 0))
    const = lambda a: pl.BlockSpec(a.shape, lambda s, c, pt, _n=a.ndim: (0,) * _n)
    in_specs = ([seq3(qm), seq3(qr), seq3(cnew), seq3(krnew), const(wukt), const(wuv)]
                + _page_specs((page, KV_LORA), layer, n_pages, g)
                + _page_specs((MLA_ROPE, page), layer, n_pages, g))
    grid_spec = pltpu.PrefetchScalarGridSpec(
        num_scalar_prefetch=1,
        grid=(n_seq, steps),
        in_specs=in_specs,
        out_specs=pl.BlockSpec((None, rows, MLA_V), lambda s, c, pt: (s, 0, 0)),
        scratch_shapes=[pltpu.VMEM((rows, 1), F32), pltpu.VMEM((rows, 1), F32), pltpu.VMEM((rows, KV_LORA), F32),
                        pltpu.VMEM((MLA_HEADS * MLA_NOPE + rows, KV_LORA), BF16),
                        pltpu.VMEM((rows, g * page), F32),
                        pltpu.VMEM((g // 2, 2 * page, KV_LORA), BF16)],
    )
    return pl.pallas_call(
        functools.partial(_mla_sample_kernel, n_new=n_new, g=g),
        grid_spec=grid_spec,
        out_shape=jax.ShapeDtypeStruct((n_seq, rows, MLA_V), F32),
        compiler_params=pltpu.CompilerParams(dimension_semantics=("arbitrary", "arbitrary"),
                                             vmem_limit_bytes=VMEM_LIMIT),
    )(page_table, qm, qr, cnew, krnew, wukt, wuv, *([cache_lat] * g), *([cache_rope] * g))


def _suffix_ones_matrix(t):
    u = np.triu(np.ones((t, t), np.float32), 1).T
    return jnp.asarray(np.concatenate([u, np.ones((t, t), np.float32)], axis=1), BF16)


def _sb_sample_kernel(pt_ref, q2_ref, knew_ref, vnew_ref, uo_ref, *rest, n_new, g):
    del pt_ref
    k_pages = rest[:g]
    v_pages = rest[g:2 * g]
    o_ref, run_ref, acc_ref = rest[2 * g:]
    step = pl.program_id(1)
    rows = q2_ref.shape[0]
    page = uo_ref.shape[0]
    q2 = q2_ref[...]

    def tiles(kts, vts, mask, run, acc):
        halves = kts[0].shape[1] // page
        zs = [_dot(q2, kt) for kt in kts]
        sps, stacks = [], []
        for z in zs:
            nz = -z
            sp = jnp.log(1.0 + jnp.exp(jnp.minimum(z, nz)))
            l1m = jnp.minimum(nz, 0.0) - sp
            if mask is not None:
                l1m = jnp.where(mask, l1m, 0.0)
            hi, lo = _split_hi_lo(l1m)
            stacks.append(jnp.concatenate([x[:, h * page:(h + 1) * page] for x in (hi, lo) for h in range(halves)],
                                          axis=0))
            sps.append(sp)
        ress = [_dot(stack, uo_ref[...]) for stack in stacks]
        a_list = []
        for z, sp, res in zip(zs, sps, ress):
            r = res[:halves * rows] + res[halves * rows:]
            tails = [None] * halves
            for h in reversed(range(halves)):
                rh = r[h * rows:(h + 1) * rows]
                tails[h] = rh[:, :page] + run
                run = run + rh[:, page:]
            tail = jnp.concatenate(tails, axis=1) if halves > 1 else tails[0]
            a = jnp.exp(jnp.minimum(z, 0.0) - sp + tail)
            if mask is not None:
                a = jnp.where(mask, a, 0.0)
            a_list.append(a.astype(BF16))
        for vt, a in zip(vts, a_list):
            acc = acc + _dot_nt(vt, a)
        return run, acc

    @pl.when(step == 0)
    def _():
        t = knew_ref.shape[1]
        qtok = lax.broadcasted_iota(jnp.int32, (rows, t), 0) % n_new
        key = lax.broadcasted_iota(jnp.int32, (rows, t), 1)
        run, acc = tiles([knew_ref[...].astype(BF16)], [vnew_ref[...].astype(BF16)], key < qtok,
                         jnp.zeros(run_ref.shape, F32), jnp.zeros(acc_ref.shape, F32))
        run_ref[...] = run
        acc_ref[...] = acc

    kts, vts = [], []
    for pair in range(g // 2):
        newer, older = 2 * pair, 2 * pair + 1
        kts.append(jnp.concatenate([k_pages[older][...].astype(BF16), k_pages[newer][...].astype(BF16)], axis=1))
        vts.append(jnp.concatenate([v_pages[older][...].astype(BF16), v_pages[newer][...].astype(BF16)], axis=1))
    run, acc = tiles(kts, vts, None, run_ref[...], acc_ref[...])
    run_ref[...] = run
    acc_ref[...] = acc

    @pl.when(step == pl.num_programs(1) - 1)
    def _():
        o_ref[...] = acc_ref[...]


def _sb_sample(layer, page_table, q2, knew, vnew, cache_k, cache_v, n_new):
    n_seq, n_pages = page_table.shape
    page = cache_k.shape[3]
    kvw = SB_KV_HEADS * SB_HEAD_DIM
    rows = q2.shape[1]
    g = _pages_per_step(n_pages)
    steps = n_pages // g
    uo = _suffix_ones_matrix(page)
    seq3 = lambda a: pl.BlockSpec((None,) + a.shape[1:], lambda s, c, pt: (s, 0, 0))
    const = lambda a: pl.BlockSpec(a.shape, lambda s, c, pt: (0, 0))
    in_specs = ([seq3(q2), seq3(knew), seq3(vnew), const(uo)]
                + _page_specs((kvw, page), layer, n_pages, g) + _page_specs((kvw, page), layer, n_pages, g))
    grid_spec = pltpu.PrefetchScalarGridSpec(
        num_scalar_prefetch=1,
        grid=(n_seq, steps),
        in_specs=in_specs,
        out_specs=pl.BlockSpec((None, kvw, rows), lambda s, c, pt: (s, 0, 0)),
        scratch_shapes=[pltpu.VMEM((rows, page), F32), pltpu.VMEM((kvw, rows), F32)],
    )
    return pl.pallas_call(
        functools.partial(_sb_sample_kernel, n_new=n_new, g=g),
        grid_spec=grid_spec,
        out_shape=jax.ShapeDtypeStruct((n_seq, kvw, rows), F32),
        compiler_params=pltpu.CompilerParams(dimension_semantics=("arbitrary", "arbitrary"),
                                             vmem_limit_bytes=VMEM_LIMIT),
    )(page_table, q2, knew, vnew, uo, *([cache_k] * g), *([cache_v] * g))


def _sigmoid(x):
    return 1.0 / (1.0 + jnp.exp(-x))


def _mix_kernel(x_ref, om_ref, yb_ref, os_ref, gmix_ref, wg_ref, wpa_ref, wpb_ref, wpc_ref, wo_ref, o_ref):
    x = x_ref[...]
    d = x.shape[1]
    h = _rms(x, gmix_ref[...]).astype(BF16)
    mix = _sigmoid(_dot(h, wg_ref[:, 0:d])) * _dot(om_ref[...], wpa_ref[...])
    mix = mix + _sigmoid(_dot(h, wg_ref[:, d:2 * d])) * _dot(yb_ref[...], wpb_ref[...])
    mix = mix + _sigmoid(_dot(h, wg_ref[:, 2 * d:3 * d])) * _dot(os_ref[...], wpc_ref[...])
    o_ref[...] = x + _dot(mix.astype(BF16), wo_ref[...])


def _mix(x, om, yb, osb, wts):
    m, d = x.shape
    tm = ROW_TILE
    ws = [wts['gmix'], wts['wg'], wts['wpa'], wts['wpb'], wts['wpc'], wts['wo']]
    return pl.pallas_call(
        _mix_kernel,
        grid=(m // tm,),
        in_specs=[_row_spec(tm, d), _row_spec(tm, om.shape[1]), _row_spec(tm, yb.shape[1]),
                  _row_spec(tm, osb.shape[1])] + [_full_spec(a) for a in ws],
        out_specs=_row_spec(tm, d),
        out_shape=jax.ShapeDtypeStruct((m, d), F32),
        compiler_params=pltpu.CompilerParams(dimension_semantics=("arbitrary",), vmem_limit_bytes=VMEM_LIMIT),
    )(x, om, yb, osb, *ws)


def _ffn_kernel(x_ref, g_ref, w1_ref, w2_ref, o_ref, *, tm, rows_per_seq, n_pad):
    x = x_ref[...]
    h = _rms(x, g_ref[...]).astype(BF16)
    a = jnp.maximum(_dot(h, w1_ref[...]), 0.0)
    y = x + _dot((a * a).astype(BF16), w2_ref[...])
    if n_pad:
        row = (pl.program_id(0) * tm) % rows_per_seq + lax.broadcasted_iota(jnp.int32, y.shape, 0)
        y = jnp.where(row >= n_pad, y, 0.0)
    o_ref[...] = y


def _ffn(x, wts, rows_per_seq, n_pad):
    m, d = x.shape
    tm = ROW_TILE
    ws = [wts['gffn'], wts['w1'], wts['w2']]
    return pl.pallas_call(
        functools.partial(_ffn_kernel, tm=tm, rows_per_seq=rows_per_seq, n_pad=n_pad),
        grid=(m // tm,),
        in_specs=[_row_spec(tm, d)] + [_full_spec(a) for a in ws],
        out_specs=_row_spec(tm, d),
        out_shape=jax.ShapeDtypeStruct((m, d), F32),
        compiler_params=pltpu.CompilerParams(dimension_semantics=("arbitrary",), vmem_limit_bytes=VMEM_LIMIT),
    )(x, *ws)


def _head_indicators(widths_and_offsets, n_cols):
    g = np.zeros((n_cols, LANE), np.float32)
    gt = np.zeros((LANE, n_cols), np.float32)
    for col, (start, width) in enumerate(widths_and_offsets):
        g[start:start + width, col] = 1.0 / width
        gt[col, start:start + width] = 1.0
    return jnp.asarray(g, BF16), jnp.asarray(np.concatenate([gt, gt], axis=0), BF16)


def _head_row(vals_per_head):
    return jnp.tile(vals_per_head, MLA_HEADS)[None, :]


def _prep_layer(l, norm_mix, w_in, q_a_norm, w_uq, kv_a_norm, w_uk, w_uv, q_norm_nope, q_norm_rope,
                k_norm_nope, k_norm_rope, conv_w, w_pa, w_pb, w_pc, w_o, norm_ffn, w_ff1, w_ff2):
    half = MLA_ROPE // 2
    d_model = w_in.shape[1]
    offs = np.cumsum((0,) + SPLIT_SIZES)
    win = w_in[l]
    kr_cols = win[:, offs[2]:offs[3]]
    zeros = lambda n: jnp.zeros((d_model, n), F32)
    kr_main = jnp.concatenate([zeros(MLA_NOPE), kr_cols, zeros(HEAD_PAD - MLA_NOPE - MLA_ROPE)], axis=1)
    kr_part = jnp.concatenate([zeros(MLA_NOPE), kr_cols[:, half:], kr_cols[:, :half],
                               zeros(HEAD_PAD - MLA_NOPE - MLA_ROPE)], axis=1)
    wa = jnp.concatenate([win[:, :offs[2]], kr_main, kr_part, win[:, offs[3]:offs[9]]], axis=1).astype(BF16)
    wg = win[:, offs[9]:].astype(BF16)

    uq = w_uq[l].reshape(Q_LORA, MLA_HEADS, MLA_NOPE + MLA_ROPE)
    zq = lambda n: jnp.zeros((Q_LORA, MLA_HEADS, n), F32)
    rope = uq[..., MLA_NOPE:]
    uq_main = jnp.concatenate([uq, zq(HEAD_PAD - MLA_NOPE - MLA_ROPE)], axis=-1)
    uq_part = jnp.concatenate([zq(MLA_NOPE), rope[..., half:], rope[..., :half],
                               zq(HEAD_PAD - MLA_NOPE - MLA_ROPE)], axis=-1)
    wuq = jnp.concatenate([uq_main.reshape(Q_LORA, _QW), uq_part.reshape(Q_LORA, _QW)], axis=1).astype(BF16)

    z_tail = jnp.zeros((HEAD_PAD - MLA_NOPE - MLA_ROPE,), F32)
    z_nope = jnp.zeros((MLA_NOPE,), F32)
    gr = q_norm_rope[l]
    gqm = _head_row(jnp.concatenate([q_norm_nope[l], gr, z_tail])) * MLA_SCALE
    gqp = _head_row(jnp.concatenate([z_nope, gr[half:], gr[:half], z_tail])) * MLA_SCALE
    gkr_v = k_norm_rope[l]
    gkr = jnp.concatenate([z_nope, gkr_v, z_tail])[None, :]
    gkrp = jnp.concatenate([z_nope, gkr_v[half:], gkr_v[:half], z_tail])[None, :]
    gk = _head_row(jnp.concatenate([k_norm_nope[l], jnp.zeros((HEAD_PAD - MLA_NOPE,), F32)]))

    q_groups = ([(h * HEAD_PAD, MLA_NOPE) for h in range(MLA_HEADS)]
                + [(h * HEAD_PAD + MLA_NOPE, MLA_ROPE) for h in range(MLA_HEADS)])
    gq_ind, gqt_ind = _head_indicators(q_groups, _QW)
    gk_ind, gkt_ind = _head_indicators([(h * HEAD_PAD, MLA_NOPE) for h in range(MLA_HEADS)], _QW)

    uk = w_uk[l]
    wuk = jnp.concatenate([uk, jnp.zeros((KV_LORA, MLA_HEADS, HEAD_PAD - MLA_NOPE), F32)], axis=-1)
    wuk = wuk.reshape(KV_LORA, _QW).astype(BF16)
    wukt = jnp.transpose(uk, (1, 2, 0)).reshape(MLA_HEADS * MLA_NOPE, KV_LORA).astype(BF16)
    ukt_pad = jnp.concatenate([jnp.transpose(uk, (1, 2, 0)),
                               jnp.zeros((MLA_HEADS, HEAD_PAD - MLA_NOPE, KV_LORA), F32)], axis=1)
    eye = jnp.eye(MLA_HEADS, dtype=F32)
    wabs = (eye[:, None, :, None] * ukt_pad[:, :, None, :]).reshape(_QW, MLA_HEADS * KV_LORA).astype(BF16)
    wuv_flat = jnp.concatenate([w_uv[l], jnp.zeros((KV_LORA, MLA_HEADS, HEAD_PAD - MLA_V), F32)], axis=-1)
    wuv_flat = wuv_flat.reshape(KV_LORA, _QW).astype(BF16)
    vone = _head_row(jnp.concatenate([jnp.zeros((MLA_V,), F32), jnp.ones((HEAD_PAD - MLA_V,), F32)]))
    wuv_heads = jnp.transpose(w_uv[l], (1, 0, 2)).astype(BF16)

    return dict(
        gmix=norm_mix[l][None, :], wa=wa, wg=wg, gqa=q_a_norm[l][None, :], wuq=wuq,
        gq_ind=gq_ind, gqt_ind=gqt_ind, gqm=gqm, gqp=gqp, gkva=kv_a_norm[l][None, :], gkr=gkr, gkrp=gkrp,
        convw=jnp.concatenate([conv_w[l], jnp.zeros((SUBLANE - CONV_K, CONV_DIM), F32)], axis=0),
        gk=gk, wuk=wuk, gk_ind=gk_ind, gkt_ind=gkt_ind, wuv=wuv_flat, vone=vone, wukt=wukt, wabs=wabs, wuv_heads=wuv_heads,
        wpa=w_pa[l].astype(BF16), wpb=w_pb[l].astype(BF16), wpc=w_pc[l].astype(BF16), wo=w_o[l].astype(BF16),
        gffn=norm_ffn[l][None, :], w1=w_ff1[l].astype(BF16), w2=w_ff2[l].astype(BF16))


def _rope_tables(pos):
    inv = ROPE_BASE ** (-jnp.arange(0, MLA_ROPE, 2, dtype=F32) / MLA_ROPE)
    ang = pos.astype(F32)[:, None] * inv[None, :]
    cos, sin = jnp.cos(ang), jnp.sin(ang)
    t = pos.shape[0]
    tail = jnp.zeros((t, HEAD_PAD - MLA_NOPE - MLA_ROPE), F32)
    cos_t = jnp.concatenate([jnp.ones((t, MLA_NOPE), F32), cos, cos, tail], axis=1)
    sin_t = jnp.concatenate([jnp.zeros((t, MLA_NOPE), F32), -sin, sin, tail], axis=1)
    return cos_t, sin_t


def kernel(x_prompt, x_sample, cache_mla_latent, cache_mla_rope, cache_sb_k, cache_sb_v, state_conv, page_table, meta_tokens, norm_mix, w_in, q_a_norm, w_uq, kv_a_norm, w_uk, w_uv, q_norm_nope, q_norm_rope, k_norm_nope, k_norm_rope, conv_w, w_pa, w_pb, w_pc, w_o, norm_ffn, w_ff1, w_ff2):
    bsz, seq, d_model = x_prompt.shape
    n_seq, n_new, _ = x_sample.shape
    depth = w_in.shape[0]
    n_pages = page_table.shape[1]
    page = cache_mla_latent.shape[2]
    past_len = n_pages * page
    assert n_new == SUBLANE and page == LANE
    l_real = seq + N_META
    lp = -(-l_real // ROW_TILE) * ROW_TILE
    n_pad = lp - l_real
    kvw = SB_KV_HEADS * SB_HEAD_DIM

    meta = jnp.broadcast_to(meta_tokens[None].astype(x_prompt.dtype), (bsz, N_META, d_model))
    xp = jnp.concatenate([jnp.zeros((bsz, n_pad, d_model), x_prompt.dtype), meta, x_prompt], axis=1)
    xp = xp.reshape(bsz * lp, d_model)
    xs = x_sample.reshape(n_seq * n_new, d_model)

    cos_p, sin_p = _rope_tables(jnp.arange(lp) - n_pad)
    cos_p = jnp.tile(cos_p, (bsz, 1))
    sin_p = jnp.tile(sin_p, (bsz, 1))
    cos_s, sin_s = _rope_tables(past_len + jnp.arange(n_new))
    cos_s = jnp.tile(cos_s, (n_seq, 1))
    sin_s = jnp.tile(sin_s, (n_seq, 1))

    cache_kt = jnp.transpose(cache_sb_k, (0, 1, 3, 4, 2)).reshape(cache_sb_k.shape[:2] + (kvw, page))
    cache_vt = jnp.transpose(cache_sb_v, (0, 1, 3, 4, 2)).reshape(cache_sb_v.shape[:2] + (kvw, page))
    cache_rope_t = jnp.transpose(cache_mla_rope, (0, 1, 3, 2))

    outs = {k: [] for k in ('p_lat', 'p_rope', 'p_k', 'p_v', 'p_conv', 's_lat', 's_rope', 's_k', 's_v', 's_conv')}
    for l in range(depth):
        wts = _prep_layer(l, norm_mix, w_in, q_a_norm, w_uq, kv_a_norm, w_uk, w_uv, q_norm_nope, q_norm_rope,
                          k_norm_nope, k_norm_rope, conv_w, w_pa, w_pb, w_pc, w_o, norm_ffn, w_ff1, w_ff2)

        q, k, v, c, kr, sk, sv, sq, yb, utail = _premix(xp, cos_p, sin_p, wts, False, lp, n_pad)
        o_m = _mla_prompt(q, k, v, bsz, lp)
        o_s = _sb_prompt(sq, sk, sv, bsz, lp, n_pad)
        xp = _ffn(_mix(xp, o_m, yb, o_s, wts), wts, lp, n_pad)
        outs['p_lat'].append(c.reshape(bsz, lp, KV_LORA)[:, n_pad:])
        outs['p_rope'].append(kr.reshape(bsz, lp, HEAD_PAD)[:, n_pad:, MLA_NOPE:MLA_NOPE + MLA_ROPE])
        outs['p_k'].append(sk.reshape(bsz, lp, SB_KV_HEADS, SB_HEAD_DIM)[:, n_pad:])
        outs['p_v'].append(sv.reshape(bsz, lp, SB_KV_HEADS, SB_HEAD_DIM)[:, n_pad:])
        outs['p_conv'].append(utail.reshape(bsz, lp // ROW_TILE, SUBLANE, CONV_DIM)[:, -1, SUBLANE - (CONV_K - 1):])

        st = state_conv[l]
        p2 = jnp.pad(st, ((0, 0), (0, n_new - (CONV_K - 1)), (0, 0))).reshape(n_seq * n_new, CONV_DIM)
        p1 = jnp.pad(st[:, 1:], ((0, 0), (0, n_new - 1), (0, 0))).reshape(n_seq * n_new, CONV_DIM)
        q, qabs, c, kr, sk, sv, sq, yb, u = _premix(xs, cos_s, sin_s, wts, True, n_new, 0, p1=p1, p2=p2)
        rows = MLA_HEADS * n_new
        qm = jnp.transpose(qabs.reshape(n_seq, n_new, MLA_HEADS, KV_LORA), (0, 2, 1, 3))
        qm = qm.reshape(n_seq, rows, KV_LORA).astype(BF16)
        qr = q.reshape(n_seq, n_new, MLA_HEADS, HEAD_PAD)[..., MLA_NOPE:MLA_NOPE + MLA_ROPE]
        qr = jnp.transpose(qr, (0, 2, 1, 3)).reshape(n_seq, rows, MLA_ROPE).astype(BF16)
        pad_new = lambda a: jnp.pad(a.reshape(n_seq, n_new, -1), ((0, 0), (0, page - n_new), (0, 0)))
        pad_new_t = lambda a: jnp.transpose(pad_new(a), (0, 2, 1))
        kr32 = kr[:, MLA_NOPE:MLA_NOPE + MLA_ROPE]
        o_m = _mla_sample(l, page_table, qm, qr, pad_new(c), pad_new_t(kr32), wts['wukt'], wts['wuv_heads'],
                          cache_mla_latent, cache_rope_t)
        o_m = jnp.transpose(o_m.reshape(n_seq, MLA_HEADS, n_new, MLA_V), (0, 2, 1, 3))
        o_m = o_m.reshape(n_seq * n_new, MLA_HEADS * MLA_V).astype(BF16)

        sq5 = jnp.transpose(sq.reshape(n_seq, n_new, SB_KV_HEADS, SB_GROUP, SB_HEAD_DIM), (0, 2, 3, 1, 4))
        zq = jnp.zeros_like(sq5[:, 0])
        q2 = jnp.stack([jnp.concatenate([sq5[:, 0], zq], axis=-1), jnp.concatenate([zq, sq5[:, 1]], axis=-1)], axis=1)
        q2 = q2.reshape(n_seq, SB_Q_HEADS * n_new, kvw).astype(BF16)
        o_s = _sb_sample(l, page_table, q2, pad_new_t(sk), pad_new_t(sv), cache_kt, cache_vt, n_new)
        o_s = o_s.reshape(n_seq, SB_KV_HEADS, SB_HEAD_DIM, SB_KV_HEADS, SB_GROUP, n_new)
        o_s = jnp.stack([o_s[:, 0, :, 0], o_s[:, 1, :, 1]], axis=1)
        o_s = jnp.transpose(o_s, (0, 4, 1, 3, 2)).reshape(n_seq * n_new, SB_Q_HEADS * SB_HEAD_DIM).astype(BF16)
        xs = _ffn(_mix(xs, o_m, yb, o_s, wts), wts, n_new, 0)
        outs['s_lat'].append(c.reshape(n_seq, n_new, KV_LORA))
        outs['s_rope'].append(kr32.reshape(n_seq, n_new, MLA_ROPE))
        outs['s_k'].append(sk.reshape(n_seq, n_new, SB_KV_HEADS, SB_HEAD_DIM))
        outs['s_v'].append(sv.reshape(n_seq, n_new, SB_KV_HEADS, SB_HEAD_DIM))
        outs['s_conv'].append(u.reshape(n_seq, n_new, CONV_DIM)[:, n_new - (CONV_K - 1):])

    y_prompt = xp.reshape(bsz, lp, d_model)[:, n_pad + N_META:]
    y_sample = xs.reshape(n_seq, n_new, d_model)
    st = lambda name: jnp.stack(outs[name])
    return (y_prompt, y_sample, st('p_lat'), st('p_rope'), st('p_k'), st('p_v'), st('p_conv'),
            st('s_lat'), st('s_rope'), st('s_k'), st('s_v'), st('s_conv'))
```

```python
import functools

import numpy as np
import jax
import jax.numpy as jnp
from jax import lax
from jax.experimental import pallas as pl
from jax.experimental.pallas import tpu as pltpu

N_META = 16
EPS = 1e-6
MLA_HEADS = 8
MLA_NOPE = 64
MLA_ROPE = 32
MLA_V = 64
Q_LORA = 384
KV_LORA = 256
ROPE_BASE = 10000.0
MLA_SCALE = (MLA_NOPE + MLA_ROPE) ** -0.5
CONV_DIM = 512
CONV_K = 3
SB_Q_HEADS = 8
SB_KV_HEADS = 2
SB_GROUP = SB_Q_HEADS // SB_KV_HEADS
SB_HEAD_DIM = 64
SB_SCALE = SB_HEAD_DIM ** -0.5
SB_QSCALE = SB_SCALE * float(np.log2(np.e))
SPLIT_SIZES = (Q_LORA, KV_LORA, MLA_ROPE, CONV_DIM, CONV_DIM, CONV_DIM,
               SB_Q_HEADS * SB_HEAD_DIM, SB_KV_HEADS * SB_HEAD_DIM, SB_KV_HEADS * SB_HEAD_DIM)

LANE = 128
SUBLANE = 8
HEAD_PAD = LANE
ROW_TILE = 256
NEG_BIG = -1e30
BIAS_LANE = MLA_NOPE + MLA_ROPE
PAD_KEY_BIAS = -30000.0
MLA_PROMPT_HEADS_PER_STEP = 4
MLA_PROMPT_TILES_PER_TRIP = 2
SB_TILES_PER_TRIP = 4
VMEM_LIMIT = 56 * 1024 * 1024

F32 = jnp.float32
BF16 = jnp.bfloat16

_A_QLAT = 0
_A_C = _A_QLAT + Q_LORA
_A_KR = _A_C + KV_LORA
_A_KRP = _A_KR + HEAD_PAD
_A_B = _A_KRP + HEAD_PAD
_A_CI = _A_B + CONV_DIM
_A_X = _A_CI + CONV_DIM
_A_SQ = _A_X + CONV_DIM
_A_SK = _A_SQ + SB_Q_HEADS * SB_HEAD_DIM
_A_SV = _A_SK + SB_KV_HEADS * SB_HEAD_DIM
_A_END = _A_SV + SB_KV_HEADS * SB_HEAD_DIM
_QW = MLA_HEADS * HEAD_PAD


def _dot(a, b):
    return jnp.dot(a, b, preferred_element_type=F32)


def _dot_nt(a, b):
    return lax.dot_general(a, b, (((1,), (1,)), ((), ())), preferred_element_type=F32)


def _rms(x, g):
    return x * lax.rsqrt(jnp.mean(x * x, axis=-1, keepdims=True) + EPS) * g


def _split_hi_lo(x):
    hi = x.astype(BF16)
    lo = (x - hi.astype(F32)).astype(BF16)
    return hi, lo


def _group_rs(raw, g_ref, gt2_ref):
    ms = _dot((raw * raw).astype(BF16), g_ref[...])
    rs = lax.rsqrt(ms + EPS)
    hi, lo = _split_hi_lo(rs)
    return _dot(jnp.concatenate([hi, lo], axis=1), gt2_ref[...])


def _premix_kernel(*refs, sample, tm, rows_per_seq, n_pad):
    if sample:
        (x_ref, cos_ref, sin_ref, p1_ref, p2_ref, gmix_ref, wa_ref, gqa_ref, wuq_ref, gq_ind, gqt_ind,
         gqm_ref, gqp_ref, gkva_ref, gkr_ref, gkrp_ref, convw_ref, gk_ref, wabs_ref,
         q_out, qabs_out, c_out, kr_out, sk_out, sv_out, sq_out, yb_out, u_out) = refs
    else:
        (x_ref, cos_ref, sin_ref, gmix_ref, wa_ref, gqa_ref, wuq_ref, gq_ind, gqt_ind,
         gqm_ref, gqp_ref, gkva_ref, gkr_ref, gkrp_ref, convw_ref, gk_ref, wuk_ref, gk_ind, gkt_ind, wuv_ref,
         vone_ref, q_out, k_out, v_out, c_out, kr_out, sk_out, sv_out, sq_out, yb_out, utail_out, carry_ref) = refs

    x = x_ref[...]
    h = _rms(x, gmix_ref[...]).astype(BF16)

    def proj(a, b):
        return _dot(h, wa_ref[:, a:b])

    cos = cos_ref[...]
    sin = sin_ref[...]

    ql = _rms(proj(_A_QLAT, _A_C), gqa_ref[...]).astype(BF16)
    qraw = _dot(ql, wuq_ref[...])
    qm = qraw[:, :_QW]
    qp = qraw[:, _QW:]
    rs_full = _group_rs(qm, gq_ind, gqt_ind)
    ym = qm * rs_full * gqm_ref[...]
    yp = qp * rs_full * gqp_ref[...]
    q_heads = []
    bias_lane = lax.broadcasted_iota(jnp.int32, (tm, HEAD_PAD), 1) == BIAS_LANE
    for hd in range(MLA_HEADS):
        sl = slice(hd * HEAD_PAD, (hd + 1) * HEAD_PAD)
        q_h = ym[:, sl] * cos + yp[:, sl] * sin
        if not sample:
            q_h = jnp.where(bias_lane, 1.0, q_h)
        q_heads.append(q_h)
    q_full = jnp.concatenate(q_heads, axis=1)
    q_out[...] = q_full.astype(q_out.dtype)

    c = _rms(proj(_A_C, _A_KR), gkva_ref[...])
    c_out[...] = c
    c_bf = c.astype(BF16)
    kr_raw = proj(_A_KR, _A_KRP)
    kr_par = proj(_A_KRP, _A_B)
    kr_rs = lax.rsqrt(jnp.sum(kr_raw * kr_raw, axis=-1, keepdims=True) * (1.0 / MLA_ROPE) + EPS)
    kr128 = (kr_raw * kr_rs * gkr_ref[...]) * cos + (kr_par * kr_rs * gkrp_ref[...]) * sin
    kr_out[...] = kr128

    if sample:
        qabs_out[...] = _dot((q_full * gk_ref[...]).astype(BF16), wabs_ref[...])
    else:
        kraw = _dot(c_bf, wuk_ref[...])
        kn = kraw * _group_rs(kraw, gk_ind, gkt_ind) * gk_ref[...]
        row_in_seq = (pl.program_id(0) * tm) % rows_per_seq + lax.broadcasted_iota(jnp.int32, (tm, HEAD_PAD), 0)
        k_tail = kr128 + jnp.where(bias_lane & (row_in_seq < n_pad), PAD_KEY_BIAS, 0.0)
        for hd in range(MLA_HEADS):
            sl = slice(hd * HEAD_PAD, (hd + 1) * HEAD_PAD)
            k_out[:, sl] = (kn[:, sl] + k_tail).astype(k_out.dtype)
        v_out[...] = (_dot(c_bf, wuv_ref[...]) + vone_ref[...]).astype(v_out.dtype)

    sq_out[...] = (proj(_A_SQ, _A_SK) * SB_QSCALE).astype(sq_out.dtype)
    sk_out[...] = proj(_A_SK, _A_SV)
    sv_out[...] = proj(_A_SV, _A_END)

    u = proj(_A_CI, _A_X) * proj(_A_X, _A_SQ)
    r1 = pltpu.roll(u, 1, 0)
    r2 = pltpu.roll(u, 2, 0)
    row = lax.broadcasted_iota(jnp.int32, u.shape, 0)
    if sample:
        in_seq = row % SUBLANE
        prev1 = jnp.where(in_seq == 0, p1_ref[...], r1)
        prev2 = jnp.where(in_seq < 2, p2_ref[...], r2)
        u_out[...] = u
    else:
        @pl.when(pl.program_id(0) == 0)
        def _():
            carry_ref[...] = jnp.zeros_like(carry_ref)
        last1 = carry_ref[SUBLANE - 1:SUBLANE, :]
        last2 = carry_ref[SUBLANE - 2:SUBLANE - 1, :]
        prev1 = jnp.where(row == 0, last1, r1)
        prev2 = jnp.where(row == 0, last2, jnp.where(row == 1, last1, r2))
        tail = u[tm - SUBLANE:, :]
        carry_ref[...] = tail
        utail_out[...] = tail
    w = convw_ref[...]
    y_conv = w[0:1, :] * prev2 + w[1:2, :] * prev1 + w[2:3, :] * u
    yb_out[...] = (proj(_A_B, _A_CI) * y_conv).astype(yb_out.dtype)


def _full_spec(a):
    nd = a.ndim
    return pl.BlockSpec(a.shape, lambda i, _nd=nd: (0,) * _nd)


def _row_spec(tm, width):
    return pl.BlockSpec((tm, width), lambda i: (i, 0))


def _premix(x, cos, sin, wts, sample, rows_per_seq, n_pad, p1=None, p2=None):
    m = x.shape[0]
    tm = ROW_TILE
    d_model = x.shape[1]
    nblk = m // tm
    row = lambda w_: _row_spec(tm, w_)
    common_w = [wts['gmix'], wts['wa'], wts['gqa'], wts['wuq'], wts['gq_ind'], wts['gqt_ind'],
                wts['gqm'], wts['gqp'], wts['gkva'], wts['gkr'], wts['gkrp'], wts['convw'], wts['gk']]
    if sample:
        ins = [x, cos, sin, p1, p2] + common_w + [wts['wabs']]
        in_specs = ([row(d_model), row(HEAD_PAD), row(HEAD_PAD), row(CONV_DIM), row(CONV_DIM)]
                    + [_full_spec(a) for a in common_w + [wts['wabs']]])
        outs = [(_QW, F32), (MLA_HEADS * KV_LORA, F32), (KV_LORA, F32), (HEAD_PAD, F32),
                (SB_KV_HEADS * SB_HEAD_DIM, F32), (SB_KV_HEADS * SB_HEAD_DIM, F32),
                (SB_Q_HEADS * SB_HEAD_DIM, F32), (CONV_DIM, BF16), (CONV_DIM, F32)]
        out_shape = [jax.ShapeDtypeStruct((m, w_), dt) for w_, dt in outs]
        out_specs = [row(w_) for w_, _ in outs]
        scratch = []
    else:
        extra = [wts['wuk'], wts['gk_ind'], wts['gkt_ind'], wts['wuv'], wts['vone']]
        ins = [x, cos, sin] + common_w + extra
        in_specs = ([row(d_model), row(HEAD_PAD), row(HEAD_PAD)] + [_full_spec(a) for a in common_w + extra])
        outs = [(_QW, BF16), (_QW, BF16), (_QW, BF16), (KV_LORA, F32), (HEAD_PAD, F32),
                (SB_KV_HEADS * SB_HEAD_DIM, F32), (SB_KV_HEADS * SB_HEAD_DIM, F32),
                (SB_Q_HEADS * SB_HEAD_DIM, BF16), (CONV_DIM, BF16)]
        out_shape = [jax.ShapeDtypeStruct((m, w_), dt) for w_, dt in outs]
        out_shape.append(jax.ShapeDtypeStruct((nblk * SUBLANE, CONV_DIM), F32))
        out_specs = [row(w_) for w_, _ in outs] + [pl.BlockSpec((SUBLANE, CONV_DIM), lambda i: (i, 0))]
        scratch = [pltpu.VMEM((SUBLANE, CONV_DIM), F32)]
    return pl.pallas_call(
        functools.partial(_premix_kernel, sample=sample, tm=tm, rows_per_seq=rows_per_seq, n_pad=n_pad),
        grid=(nblk,),
        in_specs=in_specs,
        out_specs=out_specs,
        out_shape=out_shape,
        scratch_shapes=scratch,
        compiler_params=pltpu.CompilerParams(dimension_semantics=("arbitrary",),
                                             vmem_limit_bytes=VMEM_LIMIT),
    )(*ins)


def _mla_prompt_kernel(q_ref, k_ref, v_ref, o_ref, *, tq, tk, heads):
    qi = pl.program_id(2)
    qs = [q_ref[:, hh * HEAD_PAD:(hh + 1) * HEAD_PAD] for hh in range(heads)]
    causal = lax.broadcasted_iota(jnp.int32, (tq, tk), 1) <= lax.broadcasted_iota(jnp.int32, (tq, tk), 0)

    def tiles(js, carry, diagonal):
        starts = [pl.multiple_of(j * tk, tk) for j in js]
        sls = [slice(hh * HEAD_PAD, (hh + 1) * HEAD_PAD) for hh in range(heads)]
        ss = [[_dot_nt(qs[hh], k_ref[pl.ds(ks, tk), sls[hh]]) for ks in starts] for hh in range(heads)]
        ps, scaled, m_news = [], [], []
        for hh in range(heads):
            m, acc = carry[hh]
            s_h = [jnp.where(causal, s, NEG_BIG) for s in ss[hh]] if diagonal else ss[hh]
            m_new = m
            for s in s_h:
                m_new = jnp.maximum(m_new, jnp.max(s, axis=-1, keepdims=True))
            ps.append([jnp.exp(s - m_new).astype(BF16) for s in s_h])
            scaled.append(jnp.exp(m - m_new) * acc)
            m_news.append(m_new)
        out = []
        for hh in range(heads):
            acc = scaled[hh]
            for p, ks in zip(ps[hh], starts):
                acc = acc + _dot(p, v_ref[pl.ds(ks, tk), sls[hh]])
            out.append((m_news[hh], acc))
        return tuple(out)

    nt = MLA_PROMPT_TILES_PER_TRIP
    init = tuple((jnp.full((tq, 1), NEG_BIG, F32), jnp.zeros((tq, HEAD_PAD), F32)) for _ in range(heads))
    carry = lax.fori_loop(0, qi // nt, lambda jj, c: tiles([nt * jj + i for i in range(nt)], c, False), init)
    carry = lax.fori_loop(nt * (qi // nt), qi, lambda j, c: tiles([j], c, False), carry)
    carry = tiles([qi], carry, True)
    lane = lax.broadcasted_iota(jnp.int32, (tq, HEAD_PAD), 1)
    outs = [acc / pltpu.roll(acc, MLA_V, 1) for _, acc in carry]
    for pair in range(heads // 2):
        both = jnp.where(lane < MLA_V, outs[2 * pair], pltpu.roll(outs[2 * pair + 1], MLA_V, 1))
        o_ref[:, pair * HEAD_PAD:(pair + 1) * HEAD_PAD] = both.astype(o_ref.dtype)


def _mla_prompt(q, k, v, bsz, lp):
    tq = tk = ROW_TILE
    nq = lp // tq
    heads = MLA_PROMPT_HEADS_PER_STEP
    q3 = q.reshape(bsz, lp, _QW)
    k3 = k.reshape(bsz, lp, _QW)
    v3 = v.reshape(bsz, lp, _QW)
    out = pl.pallas_call(
        functools.partial(_mla_prompt_kernel, tq=tq, tk=tk, heads=heads),
        grid=(bsz, MLA_HEADS // heads, nq),
        in_specs=[pl.BlockSpec((None, tq, heads * HEAD_PAD), lambda b, hp, i: (b, i, hp)),
                  pl.BlockSpec((None, lp, heads * HEAD_PAD), lambda b, hp, i: (b, 0, hp)),
                  pl.BlockSpec((None, lp, heads * HEAD_PAD), lambda b, hp, i: (b, 0, hp))],
        out_specs=pl.BlockSpec((None, tq, heads * MLA_V), lambda b, hp, i: (b, i, hp)),
        out_shape=jax.ShapeDtypeStruct((bsz, lp, MLA_HEADS * MLA_V), BF16),
        compiler_params=pltpu.CompilerParams(dimension_semantics=("arbitrary", "arbitrary", "arbitrary"),
                                             vmem_limit_bytes=VMEM_LIMIT),
    )(q3, k3, v3)
    return out.reshape(bsz * lp, MLA_HEADS * MLA_V)


def _log2_beta(z):
    neg_abs = lax.bitcast_convert_type(lax.bitcast_convert_type(z, jnp.uint32) | jnp.uint32(0x80000000), F32)
    sp = jnp.log2(1.0 + jnp.exp2(neg_abs))
    lb = jnp.minimum(z, 0.0) - sp
    return lb, lb - z


def _suffix_matrix2(t):
    u = np.triu(np.ones((t, t), np.float32), 1).T
    return jnp.asarray(np.concatenate([u, u], axis=0), BF16)


def _sb_prompt_kernel(q_ref, k_ref, v_ref, u_ref, o_ref, k2_ref, v2_ref, *, tq, tk, n_pad):
    kvh = pl.program_id(1)
    qi = pl.program_id(2)
    d = SB_HEAD_DIM

    @pl.when(qi == 0)
    def _():
        kk = k_ref[...]
        vv = v_ref[...]
        kroll = pltpu.roll(kk, d, 1)
        vroll = pltpu.roll(vv, d, 1)
        lane = lax.broadcasted_iota(jnp.int32, kk.shape, 1)
        first = kvh == 0
        low = lane < d
        k2_ref[:, :LANE] = jnp.where(low, jnp.where(first, kk, kroll), 0.0).astype(BF16)
        k2_ref[:, LANE:] = jnp.where(low, 0.0, jnp.where(first, kroll, kk)).astype(BF16)
        v2_ref[...] = jnp.where(low, jnp.where(first, vv, vroll), jnp.where(first, vroll, vv)).astype(BF16)

    qst = jnp.concatenate([q_ref[:, :LANE], q_ref[:, LANE:]], axis=0)
    rows = 4 * tq
    row = qi * tq + lax.broadcasted_iota(jnp.int32, (rows, tk), 0) % tq
    col0 = lax.broadcasted_iota(jnp.int32, (rows, tk), 1)
    nk = (qi * tq + tq - 1) // tk + 1

    def tiles(js, carry, masked):
        run, acc = carry
        starts = [pl.multiple_of(j * tk, tk) for j in js]
        zs, valids = [], []
        for j, ks in zip(js, starts):
            kt = k2_ref[pl.ds(ks, tk), :]
            zs.append(jnp.concatenate([_dot_nt(qst, kt[:, :LANE]), _dot_nt(qst, kt[:, LANE:])], axis=0))
            col = col0 + j * tk
            valids.append((col >= n_pad) & (col < row) if masked else None)
        lbs, tots, splits = [], [], []
        for z, valid in zip(zs, valids):
            lb, l1m = _log2_beta(z)
            if valid is not None:
                lb = jnp.where(valid, lb, NEG_BIG)
                l1m = jnp.where(valid, l1m, 0.0)
            hi, lo = _split_hi_lo(l1m)
            splits.append(jnp.concatenate([hi, lo], axis=1))
            lbs.append(lb)
            tots.append(jnp.sum(l1m, axis=-1, keepdims=True))
        sufs = [_dot(hl, u_ref[...]) for hl in splits]
        weights = []
        for lb, tot, suf in zip(lbs, tots, sufs):
            weights.append(jnp.exp2(lb + suf + run).astype(BF16))
            run = run + tot
        for ks, a in zip(starts, weights):
            acc = acc + _dot(a, v2_ref[pl.ds(ks, tk), :])
        return run, acc

    n_int = jnp.maximum(nk - 2, 0)
    nt = SB_TILES_PER_TRIP
    n_group = n_int // nt
    carry = (jnp.zeros((rows, 1), F32), jnp.zeros((rows, LANE), F32))
    carry = tiles([nk - 1], carry, True)
    carry = lax.fori_loop(0, n_group, lambda jj, c: tiles([nk - 2 - nt * jj - i for i in range(nt)], c, False), carry)
    carry = lax.fori_loop(0, n_int - nt * n_group, lambda jj, c: tiles([n_int - nt * n_group - jj], c, False), carry)
    carry = lax.cond(nk >= 2, lambda c: tiles([0], c, True), lambda c: c, carry)
    _, acc = carry
    lane = lax.broadcasted_iota(jnp.int32, (tq, LANE), 1)
    o_ref[:, :LANE] = jnp.where(lane < d, acc[0:tq], acc[2 * tq:3 * tq]).astype(o_ref.dtype)
    o_ref[:, LANE:] = jnp.where(lane < d, acc[tq:2 * tq], acc[3 * tq:]).astype(o_ref.dtype)


def _sb_prompt(sq, sk, sv, bsz, lp, n_pad):
    tq = LANE
    tk = ROW_TILE
    nq = lp // tq
    qw = SB_GROUP * SB_HEAD_DIM
    kvw = SB_KV_HEADS * SB_HEAD_DIM
    umat2 = _suffix_matrix2(tk)
    out = pl.pallas_call(
        functools.partial(_sb_prompt_kernel, tq=tq, tk=tk, n_pad=n_pad),
        grid=(bsz, SB_KV_HEADS, nq),
        in_specs=[pl.BlockSpec((None, tq, qw), lambda b, g, i: (b, i, g)),
                  pl.BlockSpec((None, lp, kvw), lambda b, g, i: (b, 0, 0)),
                  pl.BlockSpec((None, lp, kvw), lambda b, g, i: (b, 0, 0)),
                  pl.BlockSpec(umat2.shape, lambda b, g, i: (0, 0))],
        out_specs=pl.BlockSpec((None, tq, qw), lambda b, g, i: (b, i, g)),
        out_shape=jax.ShapeDtypeStruct((bsz, lp, SB_Q_HEADS * SB_HEAD_DIM), BF16),
        scratch_shapes=[pltpu.VMEM((lp, 2 * LANE), BF16), pltpu.VMEM((lp, LANE), BF16)],
        compiler_params=pltpu.CompilerParams(dimension_semantics=("arbitrary", "arbitrary", "arbitrary"),
                                             vmem_limit_bytes=VMEM_LIMIT),
    )(sq.reshape(bsz, lp, -1), sk.reshape(bsz, lp, kvw), sv.reshape(bsz, lp, kvw), umat2)
    return out.reshape(bsz * lp, SB_Q_HEADS * SB_HEAD_DIM)


PAGES_PER_STEP = 32
DMA_SLOTS = 2
N_PAGED_CACHES = 4


def _pages_per_step(n_pages):
    g = min(PAGES_PER_STEP, n_pages)
    assert g % 2 == 0 and n_pages % g == 0
    return g


def _mla_sample_body(qm_ref, qr_ref, cnew_ref, krnew_ref, wukt_ref, wuv_ref, c_pages, r_pages,
                     o_ref, m_ref, l_ref, acc_ref, w_ref, s_ref, cb_ref, n_new):
    g = len(c_pages)
    step = pl.program_id(1)
    rows = MLA_HEADS * n_new
    nk = MLA_HEADS * MLA_NOPE

    def scores(cb, kr_t):
        t = cb.shape[0]
        big = _dot_nt(w_ref[...], cb)
        kraw_t = big[:nk]
        ms = jnp.sum((kraw_t * kraw_t).reshape(MLA_HEADS, MLA_NOPE, t), axis=1) * (1.0 / MLA_NOPE)
        rs = lax.rsqrt(ms + EPS)
        s_nope = big[nk:].reshape(MLA_HEADS, n_new, t) * rs[:, None, :]
        return s_nope.reshape(rows, t) + _dot(qr_ref[...], kr_t.astype(BF16))

    def update(s, cbs):
        m_old = m_ref[...]
        m_new = jnp.maximum(m_old, jnp.max(s, axis=-1, keepdims=True))
        p = jnp.exp(s - m_new)
        alpha = jnp.exp(m_old - m_new)
        l_ref[...] = alpha * l_ref[...] + jnp.sum(p, axis=-1, keepdims=True)
        pb = p.astype(BF16)
        acc = alpha * acc_ref[...]
        off = 0
        for cb in cbs:
            t = cb.shape[0]
            acc = acc + _dot(pb[:, off:off + t], cb)
            off += t
        acc_ref[...] = acc
        m_ref[...] = m_new

    @pl.when(step == 0)
    def _():
        m_ref[...] = jnp.full_like(m_ref, NEG_BIG)
        l_ref[...] = jnp.zeros_like(l_ref)
        acc_ref[...] = jnp.zeros_like(acc_ref)
        w_ref[:nk, :] = wukt_ref[...]
        w_ref[nk:, :] = qm_ref[...]
        t = cnew_ref.shape[0]
        qtok = lax.broadcasted_iota(jnp.int32, (rows, t), 0) % n_new
        key = lax.broadcasted_iota(jnp.int32, (rows, t), 1)
        cb = cnew_ref[...].astype(BF16)
        update(jnp.where(key <= qtok, scores(cb, krnew_ref[...]), NEG_BIG), [cb])

    t2 = 2 * c_pages[0].shape[0]
    for pair in range(g // 2):
        newer, older = 2 * pair, 2 * pair + 1
        cb = jnp.concatenate([c_pages[older][...].astype(BF16), c_pages[newer][...].astype(BF16)], axis=0)
        kr = jnp.concatenate([r_pages[older][...], r_pages[newer][...]], axis=1)
        s_ref[:, pair * t2:(pair + 1) * t2] = scores(cb, kr)
        cb_ref[pair] = cb

    def finish():
        update(s_ref[...], [cb_ref[pair] for pair in range(g // 2)])

        @pl.when(step == pl.num_programs(1) - 1)
        def _():
            lat = (acc_ref[...] / l_ref[...]).astype(BF16)
            for hd in range(MLA_HEADS):
                o_ref[hd * n_new:(hd + 1) * n_new, :] = _dot(lat[hd * n_new:(hd + 1) * n_new, :], wuv_ref[hd])

    return finish


def _suffix_ones_matrix(t):
    u = np.triu(np.ones((t, t), np.float32), 1).T
    return jnp.asarray(np.concatenate([u, np.ones((t, t), np.float32)], axis=1), BF16)


def _sb_sample_body(q2_ref, knew_ref, vnew_ref, uo_ref, k_pages, v_pages, o_ref, run_ref, acc_ref, n_new):
    g = len(k_pages)
    step = pl.program_id(1)
    rows = q2_ref.shape[0]
    page = uo_ref.shape[0]
    q2 = q2_ref[...]

    def tiles(kts, vts, mask, run, acc):
        halves = kts[0].shape[1] // page
        zs = [_dot(q2, kt) for kt in kts]
        lbs, stacks = [], []
        for z in zs:
            lb, l1m = _log2_beta(z)
            if mask is not None:
                lb = jnp.where(mask, lb, NEG_BIG)
                l1m = jnp.where(mask, l1m, 0.0)
            hi, lo = _split_hi_lo(l1m)
            stacks.append(jnp.concatenate([x[:, h * page:(h + 1) * page] for x in (hi, lo) for h in range(halves)],
                                          axis=0))
            lbs.append(lb)
        ress = [_dot(stack, uo_ref[...]) for stack in stacks]
        a_list = []
        for lb, res in zip(lbs, ress):
            r = res[:halves * rows] + res[halves * rows:]
            tails = [None] * halves
            for h in reversed(range(halves)):
                rh = r[h * rows:(h + 1) * rows]
                tails[h] = rh[:, :page] + run
                run = run + rh[:, page:]
            tail = jnp.concatenate(tails, axis=1) if halves > 1 else tails[0]
            a_list.append(jnp.exp2(lb + tail).astype(BF16))
        for vt, a in zip(vts, a_list):
            acc = acc + _dot_nt(vt, a)
        return run, acc

    @pl.when(step == 0)
    def _():
        t = knew_ref.shape[1]
        qtok = lax.broadcasted_iota(jnp.int32, (rows, t), 0) % n_new
        key = lax.broadcasted_iota(jnp.int32, (rows, t), 1)
        run, acc = tiles([knew_ref[...].astype(BF16)], [vnew_ref[...].astype(BF16)], key < qtok,
                         jnp.zeros(run_ref.shape, F32), jnp.zeros(acc_ref.shape, F32))
        run_ref[...] = run
        acc_ref[...] = acc

    kts, vts = [], []
    for pair in range(g // 2):
        newer, older = 2 * pair, 2 * pair + 1
        kts.append(jnp.concatenate([k_pages[older][...].astype(BF16), k_pages[newer][...].astype(BF16)], axis=1))
        vts.append(jnp.concatenate([v_pages[older][...].astype(BF16), v_pages[newer][...].astype(BF16)], axis=1))
    run, acc = tiles(kts, vts, None, run_ref[...], acc_ref[...])
    run_ref[...] = run
    acc_ref[...] = acc

    @pl.when(step == pl.num_programs(1) - 1)
    def _():
        o_ref[...] = acc_ref[...]


def _sample_attn_kernel(pt_ref, qm_ref, qr_ref, cnew_ref, krnew_ref, wukt_ref, wuv_ref, q2_ref, knew_ref, vnew_ref,
                        uo_ref, lat_hbm, rope_hbm, k_hbm, v_hbm, om_ref, os_ref,
                        m_ref, l_ref, macc_ref, w_ref, s_ref, cb_ref, run_ref, sacc_ref,
                        cbuf, rbuf, kbuf, vbuf, sems, *, n_new, g, layer, n_pages):
    seq = pl.program_id(0)
    grp = pl.program_id(1)
    groups = pl.num_programs(1)
    t = seq * groups + grp
    slot = t % DMA_SLOTS

    def page_copies(seq_, grp_, slot_):
        copies = []
        for i in range(g):
            pid = pt_ref[seq_, n_pages - 1 - (grp_ * g + i)]
            for kind, (hbm, buf) in enumerate(((lat_hbm, cbuf), (rope_hbm, rbuf), (k_hbm, kbuf), (v_hbm, vbuf))):
                copies.append(pltpu.make_async_copy(hbm.at[layer, pid], buf.at[slot_, i], sems.at[slot_, kind]))
        return copies

    @pl.when(t == 0)
    def _():
        for cp in page_copies(seq, grp, slot):
            cp.start()

    @pl.when(t + 1 < pl.num_programs(0) * groups)
    def _():
        wrap = grp + 1 == groups
        for cp in page_copies(jnp.where(wrap, seq + 1, seq), jnp.where(wrap, 0, grp + 1), 1 - slot):
            cp.start()

    for cp in page_copies(seq, grp, slot):
        cp.wait()

    c_pages, r_pages, k_pages, v_pages = ([buf.at[slot, i] for i in range(g)] for buf in (cbuf, rbuf, kbuf, vbuf))
    finish_mla = _mla_sample_body(qm_ref, qr_ref, cnew_ref, krnew_ref, wukt_ref, wuv_ref, c_pages, r_pages,
                                  om_ref, m_ref, l_ref, macc_ref, w_ref, s_ref, cb_ref, n_new)
    _sb_sample_body(q2_ref, knew_ref, vnew_ref, uo_ref, k_pages, v_pages, os_ref, run_ref, sacc_ref, n_new)
    finish_mla()


def _sample_attention(layer, page_table, qm, qr, cnew, krnew, wukt, wuv, q2, knew, vnew,
                      cache_lat, cache_rope, cache_k, cache_v):
    n_seq, n_pages = page_table.shape
    page = cache_lat.shape[2]
    kvw = SB_KV_HEADS * SB_HEAD_DIM
    n_new = qm.shape[1] // MLA_HEADS
    rows = qm.shape[1]
    g = _pages_per_step(n_pages)
    steps = n_pages // g
    uo = _suffix_ones_matrix(page)
    seq3 = lambda a: pl.BlockSpec((None,) + a.shape[1:], lambda s, c, pt: (s, 0, 0))
    const = lambda a: pl.BlockSpec(a.shape, lambda s, c, pt, _n=a.ndim: (0,) * _n)
    in_specs = ([seq3(qm), seq3(qr), seq3(cnew), seq3(krnew), const(wukt), const(wuv),
                 seq3(q2), seq3(knew), seq3(vnew), const(uo)]
                + [pl.BlockSpec(memory_space=pl.ANY)] * 4)
    grid_spec = pltpu.PrefetchScalarGridSpec(
        num_scalar_prefetch=1,
        grid=(n_seq, steps),
        in_specs=in_specs,
        out_specs=[pl.BlockSpec((None, rows, MLA_V), lambda s, c, pt: (s, 0, 0)),
                   pl.BlockSpec((None, kvw, q2.shape[1]), lambda s, c, pt: (s, 0, 0))],
        scratch_shapes=[pltpu.VMEM((rows, 1), F32), pltpu.VMEM((rows, 1), F32), pltpu.VMEM((rows, KV_LORA), F32),
                        pltpu.VMEM((MLA_HEADS * MLA_NOPE + rows, KV_LORA), BF16),
                        pltpu.VMEM((rows, g * page), F32),
                        pltpu.VMEM((g // 2, 2 * page, KV_LORA), BF16),
                        pltpu.VMEM((q2.shape[1], page), F32), pltpu.VMEM((kvw, q2.shape[1]), F32),
                        pltpu.VMEM((DMA_SLOTS, g, page, KV_LORA), F32), pltpu.VMEM((DMA_SLOTS, g, MLA_ROPE, page), F32),
                        pltpu.VMEM((DMA_SLOTS, g, kvw, page), F32), pltpu.VMEM((DMA_SLOTS, g, kvw, page), F32),
                        pltpu.SemaphoreType.DMA((DMA_SLOTS, N_PAGED_CACHES))],
    )
    return pl.pallas_call(
        functools.partial(_sample_attn_kernel, n_new=n_new, g=g, layer=layer, n_pages=n_pages),
        grid_spec=grid_spec,
        out_shape=[jax.ShapeDtypeStruct((n_seq, rows, MLA_V), F32),
                   jax.ShapeDtypeStruct((n_seq, kvw, q2.shape[1]), F32)],
        compiler_params=pltpu.CompilerParams(dimension_semantics=("arbitrary", "arbitrary"),
                                             vmem_limit_bytes=VMEM_LIMIT),
    )(page_table, qm, qr, cnew, krnew, wukt, wuv, q2, knew, vnew, uo, cache_lat, cache_rope, cache_k, cache_v)


def _sigmoid(x):
    return 1.0 / (1.0 + jnp.exp(-x))


def _mix_kernel(x_ref, om_ref, yb_ref, os_ref, gmix_ref, wg_ref, wpa_ref, wpb_ref, wpc_ref, wo_ref, o_ref):
    x = x_ref[...]
    d = x.shape[1]
    h = _rms(x, gmix_ref[...]).astype(BF16)
    mix = _sigmoid(_dot(h, wg_ref[:, 0:d])) * _dot(om_ref[...], wpa_ref[...])
    mix = mix + _sigmoid(_dot(h, wg_ref[:, d:2 * d])) * _dot(yb_ref[...], wpb_ref[...])
    mix = mix + _sigmoid(_dot(h, wg_ref[:, 2 * d:3 * d])) * _dot(os_ref[...], wpc_ref[...])
    o_ref[...] = x + _dot(mix.astype(BF16), wo_ref[...])


def _mix(x, om, yb, osb, wts):
    m, d = x.shape
    tm = ROW_TILE
    ws = [wts['gmix'], wts['wg'], wts['wpa'], wts['wpb'], wts['wpc'], wts['wo']]
    return pl.pallas_call(
        _mix_kernel,
        grid=(m // tm,),
        in_specs=[_row_spec(tm, d), _row_spec(tm, om.shape[1]), _row_spec(tm, yb.shape[1]),
                  _row_spec(tm, osb.shape[1])] + [_full_spec(a) for a in ws],
        out_specs=_row_spec(tm, d),
        out_shape=jax.ShapeDtypeStruct((m, d), F32),
        compiler_params=pltpu.CompilerParams(dimension_semantics=("arbitrary",), vmem_limit_bytes=VMEM_LIMIT),
    )(x, om, yb, osb, *ws)


def _ffn_kernel(x_ref, g_ref, w1_ref, w2_ref, o_ref, *, tm, rows_per_seq, n_pad):
    x = x_ref[...]
    h = _rms(x, g_ref[...]).astype(BF16)
    a = jnp.maximum(_dot(h, w1_ref[...]), 0.0)
    y = x + _dot((a * a).astype(BF16), w2_ref[...])
    if n_pad:
        row = (pl.program_id(0) * tm) % rows_per_seq + lax.broadcasted_iota(jnp.int32, y.shape, 0)
        y = jnp.where(row >= n_pad, y, 0.0)
    o_ref[...] = y


def _ffn(x, wts, rows_per_seq, n_pad):
    m, d = x.shape
    tm = ROW_TILE
    ws = [wts['gffn'], wts['w1'], wts['w2']]
    return pl.pallas_call(
        functools.partial(_ffn_kernel, tm=tm, rows_per_seq=rows_per_seq, n_pad=n_pad),
        grid=(m // tm,),
        in_specs=[_row_spec(tm, d)] + [_full_spec(a) for a in ws],
        out_specs=_row_spec(tm, d),
        out_shape=jax.ShapeDtypeStruct((m, d), F32),
        compiler_params=pltpu.CompilerParams(dimension_semantics=("arbitrary",), vmem_limit_bytes=VMEM_LIMIT),
    )(x, *ws)


def _head_indicators(widths_and_offsets, n_cols):
    g = np.zeros((n_cols, LANE), np.float32)
    gt = np.zeros((LANE, n_cols), np.float32)
    for col, (start, width) in enumerate(widths_and_offsets):
        g[start:start + width, col] = 1.0 / width
        gt[col, start:start + width] = 1.0
    return jnp.asarray(g, BF16), jnp.asarray(np.concatenate([gt, gt], axis=0), BF16)


def _head_row(vals_per_head):
    return jnp.tile(vals_per_head, MLA_HEADS)[None, :]


def _prep_layer(l, norm_mix, w_in, q_a_norm, w_uq, kv_a_norm, w_uk, w_uv, q_norm_nope, q_norm_rope,
                k_norm_nope, k_norm_rope, conv_w, w_pa, w_pb, w_pc, w_o, norm_ffn, w_ff1, w_ff2):
    half = MLA_ROPE // 2
    d_model = w_in.shape[1]
    offs = np.cumsum((0,) + SPLIT_SIZES)
    win = w_in[l]
    kr_cols = win[:, offs[2]:offs[3]]
    zeros = lambda n: jnp.zeros((d_model, n), F32)
    kr_main = jnp.concatenate([zeros(MLA_NOPE), kr_cols, zeros(HEAD_PAD - MLA_NOPE - MLA_ROPE)], axis=1)
    kr_part = jnp.concatenate([zeros(MLA_NOPE), kr_cols[:, half:], kr_cols[:, :half],
                               zeros(HEAD_PAD - MLA_NOPE - MLA_ROPE)], axis=1)
    wa = jnp.concatenate([win[:, :offs[2]], kr_main, kr_part, win[:, offs[3]:offs[9]]], axis=1).astype(BF16)
    wg = win[:, offs[9]:].astype(BF16)

    uq = w_uq[l].reshape(Q_LORA, MLA_HEADS, MLA_NOPE + MLA_ROPE)
    zq = lambda n: jnp.zeros((Q_LORA, MLA_HEADS, n), F32)
    rope = uq[..., MLA_NOPE:]
    uq_main = jnp.concatenate([uq, zq(HEAD_PAD - MLA_NOPE - MLA_ROPE)], axis=-1)
    uq_part = jnp.concatenate([zq(MLA_NOPE), rope[..., half:], rope[..., :half],
                               zq(HEAD_PAD - MLA_NOPE - MLA_ROPE)], axis=-1)
    wuq = jnp.concatenate([uq_main.reshape(Q_LORA, _QW), uq_part.reshape(Q_LORA, _QW)], axis=1).astype(BF16)

    z_tail = jnp.zeros((HEAD_PAD - MLA_NOPE - MLA_ROPE,), F32)
    z_nope = jnp.zeros((MLA_NOPE,), F32)
    gr = q_norm_rope[l]
    gqm = _head_row(jnp.concatenate([q_norm_nope[l], gr, z_tail])) * MLA_SCALE
    gqp = _head_row(jnp.concatenate([z_nope, gr[half:], gr[:half], z_tail])) * MLA_SCALE
    gkr_v = k_norm_rope[l]
    gkr = jnp.concatenate([z_nope, gkr_v, z_tail])[None, :]
    gkrp = jnp.concatenate([z_nope, gkr_v[half:], gkr_v[:half], z_tail])[None, :]
    gk = _head_row(jnp.concatenate([k_norm_nope[l], jnp.zeros((HEAD_PAD - MLA_NOPE,), F32)]))

    q_groups = ([(h * HEAD_PAD, MLA_NOPE) for h in range(MLA_HEADS)]
                + [(h * HEAD_PAD + MLA_NOPE, MLA_ROPE) for h in range(MLA_HEADS)])
    gq_ind, gqt_ind = _head_indicators(q_groups, _QW)
    gk_ind, gkt_ind = _head_indicators([(h * HEAD_PAD, MLA_NOPE) for h in range(MLA_HEADS)], _QW)

    uk = w_uk[l]
    wuk = jnp.concatenate([uk, jnp.zeros((KV_LORA, MLA_HEADS, HEAD_PAD - MLA_NOPE), F32)], axis=-1)
    wuk = wuk.reshape(KV_LORA, _QW).astype(BF16)
    wukt = jnp.transpose(uk, (1, 2, 0)).reshape(MLA_HEADS * MLA_NOPE, KV_LORA).astype(BF16)
    ukt_pad = jnp.concatenate([jnp.transpose(uk, (1, 2, 0)),
                               jnp.zeros((MLA_HEADS, HEAD_PAD - MLA_NOPE, KV_LORA), F32)], axis=1)
    eye = jnp.eye(MLA_HEADS, dtype=F32)
    wabs = (eye[:, None, :, None] * ukt_pad[:, :, None, :]).reshape(_QW, MLA_HEADS * KV_LORA).astype(BF16)
    wuv_flat = jnp.concatenate([w_uv[l], jnp.zeros((KV_LORA, MLA_HEADS, HEAD_PAD - MLA_V), F32)], axis=-1)
    wuv_flat = wuv_flat.reshape(KV_LORA, _QW).astype(BF16)
    vone = _head_row(jnp.concatenate([jnp.zeros((MLA_V,), F32), jnp.ones((HEAD_PAD - MLA_V,), F32)]))
    wuv_heads = jnp.transpose(w_uv[l], (1, 0, 2)).astype(BF16)

    return dict(
        gmix=norm_mix[l][None, :], wa=wa, wg=wg, gqa=q_a_norm[l][None, :], wuq=wuq,
        gq_ind=gq_ind, gqt_ind=gqt_ind, gqm=gqm, gqp=gqp, gkva=kv_a_norm[l][None, :], gkr=gkr, gkrp=gkrp,
        convw=jnp.concatenate([conv_w[l], jnp.zeros((SUBLANE - CONV_K, CONV_DIM), F32)], axis=0),
        gk=gk, wuk=wuk, gk_ind=gk_ind, gkt_ind=gkt_ind, wuv=wuv_flat, vone=vone, wukt=wukt, wabs=wabs, wuv_heads=wuv_heads,
        wpa=w_pa[l].astype(BF16), wpb=w_pb[l].astype(BF16), wpc=w_pc[l].astype(BF16), wo=w_o[l].astype(BF16),
        gffn=norm_ffn[l][None, :], w1=w_ff1[l].astype(BF16), w2=w_ff2[l].astype(BF16))


def _rope_tables(pos):
    inv = ROPE_BASE ** (-jnp.arange(0, MLA_ROPE, 2, dtype=F32) / MLA_ROPE)
    ang = pos.astype(F32)[:, None] * inv[None, :]
    cos, sin = jnp.cos(ang), jnp.sin(ang)
    t = pos.shape[0]
    tail = jnp.zeros((t, HEAD_PAD - MLA_NOPE - MLA_ROPE), F32)
    cos_t = jnp.concatenate([jnp.ones((t, MLA_NOPE), F32), cos, cos, tail], axis=1)
    sin_t = jnp.concatenate([jnp.zeros((t, MLA_NOPE), F32), -sin, sin, tail], axis=1)
    return cos_t, sin_t


def kernel(x_prompt, x_sample, cache_mla_latent, cache_mla_rope, cache_sb_k, cache_sb_v, state_conv, page_table, meta_tokens, norm_mix, w_in, q_a_norm, w_uq, kv_a_norm, w_uk, w_uv, q_norm_nope, q_norm_rope, k_norm_nope, k_norm_rope, conv_w, w_pa, w_pb, w_pc, w_o, norm_ffn, w_ff1, w_ff2):
    bsz, seq, d_model = x_prompt.shape
    n_seq, n_new, _ = x_sample.shape
    depth = w_in.shape[0]
    n_pages = page_table.shape[1]
    page = cache_mla_latent.shape[2]
    past_len = n_pages * page
    assert n_new == SUBLANE and page == LANE
    l_real = seq + N_META
    lp = -(-l_real // ROW_TILE) * ROW_TILE
    n_pad = lp - l_real
    kvw = SB_KV_HEADS * SB_HEAD_DIM

    meta = jnp.broadcast_to(meta_tokens[None].astype(x_prompt.dtype), (bsz, N_META, d_model))
    xp = jnp.concatenate([jnp.zeros((bsz, n_pad, d_model), x_prompt.dtype), meta, x_prompt], axis=1)
    xp = xp.reshape(bsz * lp, d_model)
    xs = x_sample.reshape(n_seq * n_new, d_model)

    cos_p, sin_p = _rope_tables(jnp.arange(lp) - n_pad)
    cos_p = jnp.tile(cos_p, (bsz, 1))
    sin_p = jnp.tile(sin_p, (bsz, 1))
    cos_s, sin_s = _rope_tables(past_len + jnp.arange(n_new))
    cos_s = jnp.tile(cos_s, (n_seq, 1))
    sin_s = jnp.tile(sin_s, (n_seq, 1))

    cache_kt = jnp.transpose(cache_sb_k, (0, 1, 3, 4, 2)).reshape(cache_sb_k.shape[:2] + (kvw, page))
    cache_vt = jnp.transpose(cache_sb_v, (0, 1, 3, 4, 2)).reshape(cache_sb_v.shape[:2] + (kvw, page))
    cache_rope_t = jnp.transpose(cache_mla_rope, (0, 1, 3, 2))

    outs = {k: [] for k in ('p_lat', 'p_rope', 'p_k', 'p_v', 'p_conv', 's_lat', 's_rope', 's_k', 's_v', 's_conv')}
    for l in range(depth):
        wts = _prep_layer(l, norm_mix, w_in, q_a_norm, w_uq, kv_a_norm, w_uk, w_uv, q_norm_nope, q_norm_rope,
                          k_norm_nope, k_norm_rope, conv_w, w_pa, w_pb, w_pc, w_o, norm_ffn, w_ff1, w_ff2)

        q, k, v, c, kr, sk, sv, sq, yb, utail = _premix(xp, cos_p, sin_p, wts, False, lp, n_pad)
        o_m = _mla_prompt(q, k, v, bsz, lp)
        o_s = _sb_prompt(sq, sk, sv, bsz, lp, n_pad)
        xp = _ffn(_mix(xp, o_m, yb, o_s, wts), wts, lp, n_pad)
        outs['p_lat'].append(c.reshape(bsz, lp, KV_LORA)[:, n_pad:])
        outs['p_rope'].append(kr.reshape(bsz, lp, HEAD_PAD)[:, n_pad:, MLA_NOPE:MLA_NOPE + MLA_ROPE])
        outs['p_k'].append(sk.reshape(bsz, lp, SB_KV_HEADS, SB_HEAD_DIM)[:, n_pad:])
        outs['p_v'].append(sv.reshape(bsz, lp, SB_KV_HEADS, SB_HEAD_DIM)[:, n_pad:])
        outs['p_conv'].append(utail.reshape(bsz, lp // ROW_TILE, SUBLANE, CONV_DIM)[:, -1, SUBLANE - (CONV_K - 1):])

        st = state_conv[l]
        p2 = jnp.pad(st, ((0, 0), (0, n_new - (CONV_K - 1)), (0, 0))).reshape(n_seq * n_new, CONV_DIM)
        p1 = jnp.pad(st[:, 1:], ((0, 0), (0, n_new - 1), (0, 0))).reshape(n_seq * n_new, CONV_DIM)
        q, qabs, c, kr, sk, sv, sq, yb, u = _premix(xs, cos_s, sin_s, wts, True, n_new, 0, p1=p1, p2=p2)
        rows = MLA_HEADS * n_new
        qm = jnp.transpose(qabs.reshape(n_seq, n_new, MLA_HEADS, KV_LORA), (0, 2, 1, 3))
        qm = qm.reshape(n_seq, rows, KV_LORA).astype(BF16)
        qr = q.reshape(n_seq, n_new, MLA_HEADS, HEAD_PAD)[..., MLA_NOPE:MLA_NOPE + MLA_ROPE]
        qr = jnp.transpose(qr, (0, 2, 1, 3)).reshape(n_seq, rows, MLA_ROPE).astype(BF16)
        pad_new = lambda a: jnp.pad(a.reshape(n_seq, n_new, -1), ((0, 0), (0, page - n_new), (0, 0)))
        pad_new_t = lambda a: jnp.transpose(pad_new(a), (0, 2, 1))
        kr32 = kr[:, MLA_NOPE:MLA_NOPE + MLA_ROPE]
        sq5 = jnp.transpose(sq.reshape(n_seq, n_new, SB_KV_HEADS, SB_GROUP, SB_HEAD_DIM), (0, 2, 3, 1, 4))
        zq = jnp.zeros_like(sq5[:, 0])
        q2 = jnp.stack([jnp.concatenate([sq5[:, 0], zq], axis=-1), jnp.concatenate([zq, sq5[:, 1]], axis=-1)], axis=1)
        q2 = q2.reshape(n_seq, SB_Q_HEADS * n_new, kvw).astype(BF16)
        o_m, o_s = _sample_attention(l, page_table, qm, qr, pad_new(c), pad_new_t(kr32), wts['wukt'],
                                     wts['wuv_heads'], q2, pad_new_t(sk), pad_new_t(sv),
                                     cache_mla_latent, cache_rope_t, cache_kt, cache_vt)
        o_m = jnp.transpose(o_m.reshape(n_seq, MLA_HEADS, n_new, MLA_V), (0, 2, 1, 3))
        o_m = o_m.reshape(n_seq * n_new, MLA_HEADS * MLA_V).astype(BF16)
        o_s = o_s.reshape(n_seq, SB_KV_HEADS, SB_HEAD_DIM, SB_KV_HEADS, SB_GROUP, n_new)
        o_s = jnp.stack([o_s[:, 0, :, 0], o_s[:, 1, :, 1]], axis=1)
        o_s = jnp.transpose(o_s, (0, 4, 1, 3, 2)).reshape(n_seq * n_new, SB_Q_HEADS * SB_HEAD_DIM).astype(BF16)
        xs = _ffn(_mix(xs, o_m, yb, o_s, wts), wts, n_new, 0)
        outs['s_lat'].append(c.reshape(n_seq, n_new, KV_LORA))
        outs['s_rope'].append(kr32.reshape(n_seq, n_new, MLA_ROPE))
        outs['s_k'].append(sk.reshape(n_seq, n_new, SB_KV_HEADS, SB_HEAD_DIM))
        outs['s_v'].append(sv.reshape(n_seq, n_new, SB_KV_HEADS, SB_HEAD_DIM))
        outs['s_conv'].append(u.reshape(n_seq, n_new, CONV_DIM)[:, n_new - (CONV_K - 1):])

    y_prompt = xp.reshape(bsz, lp, d_model)[:, n_pad + N_META:]
    y_sample = xs.reshape(n_seq, n_new, d_model)
    st = lambda name: jnp.stack(outs[name])
    return (y_prompt, y_sample, st('p_lat'), st('p_rope'), st('p_k'), st('p_v'), st('p_conv'),
            st('s_lat'), st('s_rope'), st('s_k'), st('s_v'), st('s_conv'))
```
